```python
import math
import jax
import jax.numpy as jnp
from jax import lax
import numpy as np

D_MODEL = 2048
BATCH = 8
SEQ = 2048
DEPTH = 2

GRID_W = 64
CTX_LEN = 256
HEAD_DIM = 128
N_MIXERS = 4
GROUP_HEADS = D_MODEL // HEAD_DIM // N_MIXERS
GROUP_WIDTH = GROUP_HEADS * HEAD_DIM
MIX_WIDTH = N_MIXERS * GROUP_WIDTH
GQA_HEADS = GROUP_HEADS
GQA_KV_HEADS = GROUP_HEADS // 2
NA_HEADS = GROUP_HEADS
WIN_ROWS = 8
WIN_COLS = 16
DIFF_HEADS = GROUP_HEADS
DIFF_QK_DIM = HEAD_DIM // 2
SWA_HEADS = GROUP_HEADS
SWA_KV_HEADS = GROUP_HEADS // 2
WINDOW = 128
Q_BLOCK = 128
MLP_HIDDEN = 4 * D_MODEL
N_MOD = 6
ROPE_THETA = 10000.0
NORM_EPS = 1e-6
NEG_INF = -1e30
PROJ_HEADS = (GQA_HEADS, GQA_KV_HEADS, GQA_KV_HEADS, NA_HEADS, NA_HEADS, NA_HEADS, 2 * DIFF_HEADS, 2 * DIFF_HEADS, DIFF_HEADS, SWA_HEADS, SWA_KV_HEADS, SWA_KV_HEADS)
PROJ_DIMS = (HEAD_DIM, HEAD_DIM, HEAD_DIM, HEAD_DIM, HEAD_DIM, HEAD_DIM, DIFF_QK_DIM, DIFF_QK_DIM, HEAD_DIM, HEAD_DIM, HEAD_DIM, HEAD_DIM)
PROJ_WIDTH = sum(h * d for h, d in zip(PROJ_HEADS, PROJ_DIMS))

kernel_name = 'hybrid_parallel_group_flow_block'


def rms_norm(x, g):
    xf = x.astype(jnp.float32)
    y = xf * lax.rsqrt(jnp.mean(xf * xf, axis=-1, keepdims=True) + NORM_EPS)
    return (y * g.astype(jnp.float32)).astype(x.dtype)


def modulate(h, shift, scale):
    return h * (1 + scale) + shift


def axial_rope(n_tok, dim):
    t = jnp.arange(n_tok)
    row = (t // GRID_W).astype(jnp.float32)
    col = (t % GRID_W).astype(jnp.float32)
    n_freq = dim // 4
    inv_freq = ROPE_THETA ** (-jnp.arange(n_freq, dtype=jnp.float32) / n_freq)
    ang = jnp.concatenate([row[:, None] * inv_freq, col[:, None] * inv_freq], axis=-1)
    return jnp.cos(ang), jnp.sin(ang)


def apply_rope(x, cos, sin):
    xf = x.astype(jnp.float32).reshape(*x.shape[:-1], -1, 2)
    x0, x1 = xf[..., 0], xf[..., 1]
    c = cos[None, :, None, :]
    s = sin[None, :, None, :]
    return jnp.stack([x0 * c - x1 * s, x0 * s + x1 * c], axis=-1).reshape(x.shape).astype(x.dtype)


def split_heads(p):
    parts, start = [], 0
    for h, d in zip(PROJ_HEADS, PROJ_DIMS):
        parts.append(p[..., start:start + h * d].reshape(*p.shape[:-1], h, d))
        start += h * d
    return parts


def to_blocks(a, size):
    b, s = a.shape[:2]
    return jnp.swapaxes(a.reshape(b, s // size, size, *a.shape[2:]), 0, 1)


def from_blocks(a):
    nb, b, size = a.shape[:3]
    return jnp.swapaxes(a, 0, 1).reshape(b, nb * size, *a.shape[3:])


def gqa_attend(q, k, v, bias=None, sink=None):
    b, nq, h, d = q.shape
    hkv = k.shape[2]
    g = h // hkv
    qg = q.reshape(b, nq, hkv, g, d)
    s = jnp.einsum('bqhgd,bkhd->bhgqk', qg, k).astype(jnp.float32) * (d ** -0.5)
    if bias is not None:
        s = s + bias
    if sink is not None:
        sk = jnp.broadcast_to(sink.astype(jnp.float32).reshape(1, hkv, g, 1, 1), s.shape[:-1] + (1,))
        p = jax.nn.softmax(jnp.concatenate([s, sk], axis=-1), axis=-1)[..., :-1]
    else:
        p = jax.nn.softmax(s, axis=-1)
    o = jnp.einsum('bhgqk,bkhd->bqhgd', p.astype(v.dtype), v)
    return o.reshape(b, nq, h * v.shape[-1])


def mixer_global_gqa(q, k, v, qc, kc, vc, gq, gk, cos, sin, with_ctx):
    q = apply_rope(rms_norm(q, gq), cos, sin)
    k = apply_rope(rms_norm(k, gk), cos, sin)
    kc = rms_norm(kc, gk)
    k_all = jnp.concatenate([k, kc], axis=1)
    v_all = jnp.concatenate([v, vc], axis=1)
    out = from_blocks(lax.map(lambda qb: gqa_attend(qb, k_all, v_all), to_blocks(q, Q_BLOCK)))
    out_c = gqa_attend(rms_norm(qc, gq), kc, vc) if with_ctx else None
    return out, out_c


def mixer_neighbourhood(q, k, v, qc, kc, vc, rpb, with_ctx):
    b, s, h, d = q.shape
    rows = s // GRID_W
    wr = min(WIN_ROWS, rows)
    wc = min(WIN_COLS, GRID_W)
    n_nb = wr * wc
    scale = d ** -0.5
    k_grid = k.reshape(b, rows, GRID_W, h, d)
    v_grid = v.reshape(b, rows, GRID_W, h, d)
    col = jnp.arange(GRID_W)
    col_idx = jnp.clip(col - wc // 2, 0, GRID_W - wc)[:, None] + jnp.arange(wc)[None, :]
    dcol = col_idx - col[:, None] + WIN_COLS - 1

    def row_block(args):
        r, qr = args
        r_start = jnp.clip(r - wr // 2, 0, rows - wr)
        kr = lax.dynamic_slice_in_dim(k_grid, r_start, wr, axis=1)[:, :, col_idx]
        vr = lax.dynamic_slice_in_dim(v_grid, r_start, wr, axis=1)[:, :, col_idx]
        drow = r_start + jnp.arange(wr) - r + WIN_ROWS - 1
        bias = rpb[:, drow[None, :, None], dcol[:, None, :]].astype(jnp.float32)
        s_nb = jnp.einsum('bqhd,biqjhd->bhqij', qr, kr).astype(jnp.float32) * scale + bias[None]
        s_cx = jnp.einsum('bqhd,bkhd->bhqk', qr, kc).astype(jnp.float32) * scale
        p = jax.nn.softmax(jnp.concatenate([s_nb.reshape(b, h, GRID_W, n_nb), s_cx], axis=-1), axis=-1).astype(v.dtype)
        o = (jnp.einsum('bhqij,biqjhd->bqhd', p[..., :n_nb].reshape(b, h, GRID_W, wr, wc), vr)
             + jnp.einsum('bhqk,bkhd->bqhd', p[..., n_nb:], vc))
        return o.reshape(b, GRID_W, h * d)

    out = from_blocks(lax.map(row_block, (jnp.arange(rows), to_blocks(q, GRID_W))))
    out_c = gqa_attend(qc, kc, vc) if with_ctx else None
    return out, out_c


def diff_attend(q, k, v, lam):
    s = jnp.einsum('bqhd,bkhd->bhqk', q, k).astype(jnp.float32) * (q.shape[-1] ** -0.5)
    a = jax.nn.softmax(s, axis=-1)
    p = (a[:, 0::2] - lam * a[:, 1::2]).astype(v.dtype)
    return jnp.einsum('bhqk,bkhd->bqhd', p, v)


def mixer_diff(q, k, v, qc, kc, vc, lq1, lk1, lq2, lk2, gsub, lam_init, cos, sin, with_ctx):
    f32 = jnp.float32
    lam = (jnp.exp(jnp.sum(lq1.astype(f32) * lk1.astype(f32)))
           - jnp.exp(jnp.sum(lq2.astype(f32) * lk2.astype(f32))) + lam_init)
    q = apply_rope(q, cos, sin)
    k = apply_rope(k, cos, sin)
    k_all = jnp.concatenate([k, kc], axis=1)
    v_all = jnp.concatenate([v, vc], axis=1)

    def post(o):
        return (rms_norm(o, gsub) * (1 - lam_init)).reshape(o.shape[0], o.shape[1], -1)

    out = from_blocks(lax.map(lambda qb: post(diff_attend(qb, k_all, v_all, lam)), to_blocks(q, Q_BLOCK)))
    out_c = post(diff_attend(qc, kc, vc, lam)) if with_ctx else None
    return out, out_c


def mixer_window_gqa(q, k, v, qc, kc, vc, sink, cos, sin, with_ctx):
    b, s = q.shape[:2]
    span = Q_BLOCK + 2 * WINDOW
    q = apply_rope(q, cos, sin)
    k = apply_rope(k, cos, sin)
    pad = ((0, 0), (WINDOW, WINDOW), (0, 0), (0, 0))
    k_pad = jnp.pad(k, pad)
    v_pad = jnp.pad(v, pad)
    ctx_bias = jnp.zeros((Q_BLOCK, kc.shape[1]), jnp.float32)

    def block(args):
        i, qb = args
        start = i * Q_BLOCK
        kb = lax.dynamic_slice_in_dim(k_pad, start, span, axis=1)
        vb = lax.dynamic_slice_in_dim(v_pad, start, span, axis=1)
        q_pos = start + jnp.arange(Q_BLOCK)
        k_pos = start - WINDOW + jnp.arange(span)
        valid = (k_pos[None, :] >= 0) & (k_pos[None, :] < s) & (jnp.abs(q_pos[:, None] - k_pos[None, :]) <= WINDOW)
        bias = jnp.concatenate([jnp.where(valid, 0.0, NEG_INF), ctx_bias], axis=-1)
        return gqa_attend(qb, jnp.concatenate([kb, kc], axis=1), jnp.concatenate([vb, vc], axis=1), bias=bias, sink=sink)

    out = from_blocks(lax.map(block, (jnp.arange(s // Q_BLOCK), to_blocks(q, Q_BLOCK))))
    out_c = gqa_attend(qc, kc, vc, sink=sink) if with_ctx else None
    return out, out_c


def squared_relu_mlp(h, w_up, w_down):
    return jnp.square(jax.nn.relu(h @ w_up)) @ w_down


def setup_inputs(seed: int = 0) -> dict:
    key = jax.random.key(seed)
    ks = iter(jax.random.split(key, 24))

    def nrm(shape, scale=1.0):
        return jax.random.normal(next(ks), shape, jnp.float32) * scale

    def gain(shape):
        return 1.0 + 0.1 * nrm(shape)

    L = DEPTH
    return {
        'x': nrm((BATCH, SEQ, D_MODEL)),
        'c': nrm((BATCH, D_MODEL)),
        'ctx': nrm((BATCH, CTX_LEN, D_MODEL)),
        'c_ctx': nrm((D_MODEL,)),
        'g_mix': gain((L, D_MODEL)),
        'g_mlp': gain((L, D_MODEL)),
        'w_mod': nrm((L, D_MODEL, N_MOD * D_MODEL), 0.5 * D_MODEL ** -0.5),
        'b_mod': nrm((L, N_MOD * D_MODEL), 0.02),
        'w_in': nrm((L, D_MODEL, PROJ_WIDTH), D_MODEL ** -0.5),
        'w_out': nrm((L, MIX_WIDTH, D_MODEL), MIX_WIDTH ** -0.5),
        'gqa_gq': gain((L, HEAD_DIM)),
        'gqa_gk': gain((L, HEAD_DIM)),
        'na_rpb': nrm((L, NA_HEADS, 2 * WIN_ROWS - 1, 2 * WIN_COLS - 1), 0.1),
        'diff_lq1': nrm((L, DIFF_QK_DIM), 0.1),
        'diff_lk1': nrm((L, DIFF_QK_DIM), 0.1),
        'diff_lq2': nrm((L, DIFF_QK_DIM), 0.1),
        'diff_lk2': nrm((L, DIFF_QK_DIM), 0.1),
        'diff_gsub': gain((L, HEAD_DIM)),
        'swa_sink': nrm((L, SWA_HEADS), 0.5),
        'w_up': nrm((L, D_MODEL, MLP_HIDDEN), D_MODEL ** -0.5),
        'w_down': nrm((L, MLP_HIDDEN, D_MODEL), MLP_HIDDEN ** -0.5),
        'g_final': gain((D_MODEL,)),
    }


def reference(x, c, ctx, c_ctx, g_mix, g_mlp, w_mod, b_mod, w_in, w_out, gqa_gq, gqa_gk, na_rpb,
              diff_lq1, diff_lk1, diff_lq2, diff_lk2, diff_gsub, swa_sink, w_up, w_down, g_final):
    b, s, _ = x.shape
    cos_h, sin_h = axial_rope(s, HEAD_DIM)
    cos_d, sin_d = axial_rope(s, DIFF_QK_DIM)
    cond_x = jax.nn.silu(c)
    cond_c = jax.nn.silu(c_ctx)
    for l in range(DEPTH):
        with_ctx = l < DEPTH - 1
        mod_x = (cond_x @ w_mod[l] + b_mod[l]).reshape(b, N_MOD, 1, D_MODEL)
        mod_c = (cond_c @ w_mod[l] + b_mod[l]).reshape(N_MOD, 1, D_MODEL)
        hx = modulate(rms_norm(x, g_mix[l]), mod_x[:, 0], mod_x[:, 1])
        hc = modulate(rms_norm(ctx, g_mix[l]), mod_c[0], mod_c[1])
        (aq, ak, av, bq, bk, bv, cq, ck, cv, dq, dk, dv) = split_heads(hx @ w_in[l])
        (aqc, akc, avc, bqc, bkc, bvc, cqc, ckc, cvc, dqc, dkc, dvc) = split_heads(hc @ w_in[l])
        oa, oa_c = mixer_global_gqa(aq, ak, av, aqc, akc, avc, gqa_gq[l], gqa_gk[l], cos_h, sin_h, with_ctx)
        ob, ob_c = mixer_neighbourhood(bq, bk, bv, bqc, bkc, bvc, na_rpb[l], with_ctx)
        oc, oc_c = mixer_diff(cq, ck, cv, cqc, ckc, cvc, diff_lq1[l], diff_lk1[l], diff_lq2[l], diff_lk2[l],
                              diff_gsub[l], 0.8 - 0.6 * math.exp(-0.3 * l), cos_d, sin_d, with_ctx)
        od, od_c = mixer_window_gqa(dq, dk, dv, dqc, dkc, dvc, swa_sink[l], cos_h, sin_h, with_ctx)
        x = x + mod_x[:, 2] * (jnp.concatenate([oa, ob, oc, od], axis=-1) @ w_out[l])
        x = x + mod_x[:, 5] * squared_relu_mlp(modulate(rms_norm(x, g_mlp[l]), mod_x[:, 3], mod_x[:, 4]), w_up[l], w_down[l])
        if with_ctx:
            ctx = ctx + mod_c[2] * (jnp.concatenate([oa_c, ob_c, oc_c, od_c], axis=-1) @ w_out[l])
            ctx = ctx + mod_c[5] * squared_relu_mlp(modulate(rms_norm(ctx, g_mlp[l]), mod_c[3], mod_c[4]), w_up[l], w_down[l])
    return rms_norm(x, g_final)
```

```python
import functools
import math

import jax
import jax.numpy as jnp
from jax import lax
from jax.experimental import pallas as pl
from jax.experimental.pallas import tpu as pltpu

D_MODEL = 2048
SEQ = 2048
DEPTH = 2
GRID_W = 64
CTX_LEN = 256
HEAD_DIM = 128
GROUP_WIDTH = 512
WIN_ROWS = 8
WIN_COLS = 16
DIFF_QK_DIM = 64
WINDOW = 128
MLP_HIDDEN = 4 * D_MODEL
N_MOD = 6
ROPE_THETA = 10000.0
NORM_EPS = 1e-6
NEG_INF = -1e30
PROJ_WIDTH = 5120

LANES = 128
COL_TILE = 512
MOD_ROWS = 16
CTX_MOD_ROW = 8

F32 = jnp.float32
BF16 = jnp.bfloat16

SEG = dict(aq=0, ak=4, av=6, bq=8, bk=12, bv=16, cq=20, ck=24, cv=28, dq=32, dk=36, dv=38)
ALL_TILES = tuple(range(PROJ_WIDTH // COL_TILE))
KV_TILES = (1, 3, 4, 6, 7, 9)
_ROPE_H, _ROPE_D = 0, 1
BLOCK_KIND = {}
for _b in range(PROJ_WIDTH // LANES):
    if _b < 4:
        BLOCK_KIND[_b] = (0, _ROPE_H)
    elif _b < 6:
        BLOCK_KIND[_b] = (1, _ROPE_H)
    elif 20 <= _b < 28:
        BLOCK_KIND[_b] = (None, _ROPE_D)
    elif 32 <= _b < 38:
        BLOCK_KIND[_b] = (None, _ROPE_H)
    else:
        BLOCK_KIND[_b] = (None, None)


def _seg_block(tiles, name):
    b = SEG[name]
    t, r = divmod(b, COL_TILE // LANES)
    return tiles.index(t) * (COL_TILE // LANES) + r


def _params(dims, vmem_mb):
    return pltpu.CompilerParams(dimension_semantics=dims, vmem_limit_bytes=vmem_mb * 1024 * 1024)


def _rms(x, g):
    ms = jnp.mean(x * x, axis=-1, keepdims=True)
    return x * lax.rsqrt(ms + NORM_EPS) * g


def _dot(a, b):
    return jnp.dot(a, b, preferred_element_type=F32)


def _dot_t(a, b):
    return lax.dot_general(a, b, (((1,), (1,)), ((), ())), preferred_element_type=F32)


MOD_TN = 1024


def _mod_kernel(c_ref, w_ref, b_ref, o_ref):
    c = c_ref[...]
    cond = c * (1.0 / (1.0 + jnp.exp(-c)))
    o_ref[0] = _dot(cond.astype(BF16), w_ref[0].astype(BF16)) + b_ref[0]


def _mod_call(c_rows, w_mod, b_mod):
    n = N_MOD * D_MODEL
    return pl.pallas_call(
        _mod_kernel,
        grid=(DEPTH, n // MOD_TN),
        in_specs=[
            pl.BlockSpec((MOD_ROWS, D_MODEL), lambda l, j: (0, 0)),
            pl.BlockSpec((1, D_MODEL, MOD_TN), lambda l, j: (l, 0, j)),
            pl.BlockSpec((1, 1, MOD_TN), lambda l, j: (l, 0, j)),
        ],
        out_specs=pl.BlockSpec((1, MOD_ROWS, MOD_TN), lambda l, j: (l, 0, j)),
        out_shape=jax.ShapeDtypeStruct((DEPTH, MOD_ROWS, n), F32),
        compiler_params=_params(("arbitrary", "arbitrary"), 40),
        name="mod",
    )(c_rows, w_mod, b_mod.reshape(DEPTH, 1, n))


PROJ_TM = 512


def _proj_kernel(x_ref, shift_ref, scale_ref, g_ref, w_ref, rope_ref, hg_ref, o_ref, *, tiles):
    h = _rms(x_ref[...], g_ref[...]) * (1.0 + scale_ref[0]) + shift_ref[0]
    hb = h.astype(BF16)
    per = COL_TILE // LANES
    for n, t in enumerate(tiles):
        y = _dot(hb, w_ref[:, t * COL_TILE:(t + 1) * COL_TILE])
        for j in range(per):
            gain_row, rope = BLOCK_KIND[t * per + j]
            yj = y[:, j * LANES:(j + 1) * LANES]
            if gain_row is not None:
                yj = _rms(yj, hg_ref[gain_row:gain_row + 1, :])
            if rope is not None:
                base = rope * 3 * LANES
                cos = rope_ref[:, base:base + LANES]
                sin_next = rope_ref[:, base + LANES:base + 2 * LANES]
                sin_prev = rope_ref[:, base + 2 * LANES:base + 3 * LANES]
                yj = (yj * cos + pltpu.roll(yj, LANES - 1, 1) * sin_next
                      + pltpu.roll(yj, 1, 1) * sin_prev)
            o_ref[:, (n * per + j) * LANES:(n * per + j + 1) * LANES] = yj.astype(BF16)


def _proj_call(x2d, modl, g, w_in, rope, head_gains, *, tiles, mod_row, rope_tile):
    t_tok = x2d.shape[0]
    tm = PROJ_TM
    wout = len(tiles) * COL_TILE
    return pl.pallas_call(
        functools.partial(_proj_kernel, tiles=tiles),
        grid=(t_tok // tm,),
        in_specs=[
            pl.BlockSpec((tm, D_MODEL), lambda i: (i, 0)),
            pl.BlockSpec((1, 1, D_MODEL), lambda i: (mod_row(i) * N_MOD + 0, 0, 0)),
            pl.BlockSpec((1, 1, D_MODEL), lambda i: (mod_row(i) * N_MOD + 1, 0, 0)),
            pl.BlockSpec((1, D_MODEL), lambda i: (0, 0)),
            pl.BlockSpec((D_MODEL, PROJ_WIDTH), lambda i: (0, 0), pipeline_mode=pl.Buffered(1)),
            pl.BlockSpec((tm, 6 * LANES), lambda i: (rope_tile(i), 0)),
            pl.BlockSpec((8, LANES), lambda i: (0, 0)),
        ],
        out_specs=pl.BlockSpec((tm, wout), lambda i: (i, 0)),
        out_shape=jax.ShapeDtypeStruct((t_tok, wout), BF16),
        compiler_params=_params(("arbitrary",), 56),
        name="proj",
    )(x2d, modl, modl, g, w_in, rope, head_gains)


A_TQ = 512


def _attn_a_kernel(q_ref, k_ref, v_ref, kc_ref, vc_ref, o_ref):
    scale = HEAD_DIM ** -0.5
    k, v, kc, vc = k_ref[0], v_ref[0], kc_ref[0], vc_ref[0]
    for g in range(2):
        q = q_ref[0, :, g * LANES:(g + 1) * LANES]
        sx = _dot_t(q, k) * scale
        sc = _dot_t(q, kc) * scale
        m = jnp.maximum(jnp.max(sx, axis=-1, keepdims=True), jnp.max(sc, axis=-1, keepdims=True))
        ex = jnp.exp(sx - m)
        ec = jnp.exp(sc - m)
        den = jnp.sum(ex, axis=-1, keepdims=True) + jnp.sum(ec, axis=-1, keepdims=True)
        o = _dot(ex.astype(BF16), v) + _dot(ec.astype(BF16), vc)
        o_ref[0, :, g * LANES:(g + 1) * LANES] = (o / den).astype(BF16)


def _attn_a_call(px, pc, ctx_tiles):
    b = px.shape[0]
    ak, av = SEG["ak"], SEG["av"]
    akc, avc = _seg_block(ctx_tiles, "ak"), _seg_block(ctx_tiles, "av")
    return pl.pallas_call(
        _attn_a_kernel,
        grid=(b, 2, SEQ // A_TQ),
        in_specs=[
            pl.BlockSpec((1, A_TQ, 2 * LANES), lambda i, h, t: (i, t, h)),
            pl.BlockSpec((1, SEQ, LANES), lambda i, h, t: (i, 0, ak + h)),
            pl.BlockSpec((1, SEQ, LANES), lambda i, h, t: (i, 0, av + h)),
            pl.BlockSpec((1, CTX_LEN, LANES), lambda i, h, t: (i, 0, akc + h)),
            pl.BlockSpec((1, CTX_LEN, LANES), lambda i, h, t: (i, 0, avc + h)),
        ],
        out_specs=pl.BlockSpec((1, A_TQ, 2 * LANES), lambda i, h, t: (i, t, h)),
        out_shape=jax.ShapeDtypeStruct((b, SEQ, GROUP_WIDTH), BF16),
        compiler_params=_params(("arbitrary",) * 3, 48),
        name="attn_a",
    )(px, px, px, pc, pc)


NB_ROWS = SEQ // GRID_W
NB_KEYS = WIN_ROWS * GRID_W


def _na_bias_table(rpb):
    c = jnp.arange(GRID_W)
    c_start = jnp.clip(c - WIN_COLS // 2, 0, GRID_W - WIN_COLS)
    kc = jnp.arange(GRID_W)
    in_win = (kc[None, :] >= c_start[:, None]) & (kc[None, :] < c_start[:, None] + WIN_COLS)
    dcol = jnp.clip(kc[None, :] - c[:, None] + WIN_COLS - 1, 0, 2 * WIN_COLS - 2)
    off = jnp.arange(WIN_ROWS)
    drow = jnp.arange(WIN_ROWS)[None, :] - off[:, None] + WIN_ROWS - 1
    bias = rpb.astype(F32)[:, drow[:, :, None, None], dcol[None, None, :, :]]
    bias = jnp.where(in_win[None, None, None], bias, NEG_INF)
    return bias.transpose(0, 1, 3, 2, 4).reshape(rpb.shape[0], WIN_ROWS, GRID_W, NB_KEYS)


def _attn_b_kernel(q_ref, k_ref, v_ref, kc_ref, vc_ref, bias_ref, o_ref):
    scale = HEAD_DIM ** -0.5
    kc, vc = kc_ref[0], vc_ref[0]

    def row(r, carry):
        r_start = jnp.clip(r - WIN_ROWS // 2, 0, NB_ROWS - WIN_ROWS)
        q0 = pl.multiple_of(r * GRID_W, GRID_W)
        k0 = pl.multiple_of(r_start * GRID_W, GRID_W)
        q = q_ref[0, pl.ds(q0, GRID_W), :]
        kw = k_ref[0, pl.ds(k0, NB_KEYS), :]
        vw = v_ref[0, pl.ds(k0, NB_KEYS), :]
        sn = _dot_t(q, kw) * scale + bias_ref[0, r - r_start]
        sc = _dot_t(q, kc) * scale
        m = jnp.maximum(jnp.max(sn, axis=-1, keepdims=True), jnp.max(sc, axis=-1, keepdims=True))
        en = jnp.exp(sn - m)
        ec = jnp.exp(sc - m)
        den = jnp.sum(en, axis=-1, keepdims=True) + jnp.sum(ec, axis=-1, keepdims=True)
        o = _dot(en.astype(BF16), vw) + _dot(ec.astype(BF16), vc)
        o_ref[0, pl.ds(q0, GRID_W), :] = (o / den).astype(BF16)
        return carry

    lax.fori_loop(0, NB_ROWS, row, 0)


def _attn_b_call(px, pc, ctx_tiles, bias):
    b = px.shape[0]
    bq, bk, bv = SEG["bq"], SEG["bk"], SEG["bv"]
    bkc, bvc = _seg_block(ctx_tiles, "bk"), _seg_block(ctx_tiles, "bv")
    return pl.pallas_call(
        _attn_b_kernel,
        grid=(b, 4),
        in_specs=[
            pl.BlockSpec((1, SEQ, LANES), lambda i, h: (i, 0, bq + h)),
            pl.BlockSpec((1, SEQ, LANES), lambda i, h: (i, 0, bk + h)),
            pl.BlockSpec((1, SEQ, LANES), lambda i, h: (i, 0, bv + h)),
            pl.BlockSpec((1, CTX_LEN, LANES), lambda i, h: (i, 0, bkc + h)),
            pl.BlockSpec((1, CTX_LEN, LANES), lambda i, h: (i, 0, bvc + h)),
            pl.BlockSpec((1, WIN_ROWS, GRID_W, NB_KEYS), lambda i, h: (h, 0, 0, 0)),
        ],
        out_specs=pl.BlockSpec((1, SEQ, LANES), lambda i, h: (i, 0, h)),
        out_shape=jax.ShapeDtypeStruct((b, SEQ, GROUP_WIDTH), BF16),
        compiler_params=_params(("arbitrary",) * 2, 32),
        name="attn_b",
    )(px, px, px, pc, pc, bias)


C_TQ = 512


def _diff_lambda(lam_ref, lam_init):
    a = jnp.sum(lam_ref[0:1, :] * lam_ref[1:2, :], axis=-1, keepdims=True)
    b = jnp.sum(lam_ref[2:3, :] * lam_ref[3:4, :], axis=-1, keepdims=True)
    return jnp.exp(a) - jnp.exp(b) + lam_init


def _diff_head(q, keys, vals, lam, gsub, lam_init):
    scale = DIFF_QK_DIM ** -0.5
    lane = lax.broadcasted_iota(jnp.int32, q.shape, 1)
    zero = jnp.zeros_like(q)
    probs = []
    for sub in range(2):
        qs = jnp.where((lane < DIFF_QK_DIM) == (sub == 0), q, zero)
        s = [_dot_t(qs, k) * scale for k in keys]
        m = functools.reduce(jnp.maximum, [jnp.max(x, axis=-1, keepdims=True) for x in s])
        e = [jnp.exp(x - m) for x in s]
        den = functools.reduce(jnp.add, [jnp.sum(x, axis=-1, keepdims=True) for x in e])
        probs.append((e, 1.0 / den))
    (e1, r1), (e2, r2) = probs
    r2 = lam * r2
    o = functools.reduce(jnp.add, [
        _dot((a * r1 - b * r2).astype(BF16), v) for a, b, v in zip(e1, e2, vals)])
    return _rms(o, gsub) * (1.0 - lam_init)


def _attn_c_kernel(lam_ref, gsub_ref, q_ref, k_ref, v_ref, kc_ref, vc_ref, o_ref, *, lam_init):
    lam = _diff_lambda(lam_ref, lam_init)
    o = _diff_head(q_ref[0], [k_ref[0], kc_ref[0]], [v_ref[0], vc_ref[0]], lam, gsub_ref[...], lam_init)
    o_ref[0] = o.astype(BF16)


def _attn_c_call(px, pc, ctx_tiles, lam_rows, gsub, lam_init):
    b = px.shape[0]
    cq, ck, cv = SEG["cq"], SEG["ck"], SEG["cv"]
    ckc, cvc = _seg_block(ctx_tiles, "ck"), _seg_block(ctx_tiles, "cv")
    return pl.pallas_call(
        functools.partial(_attn_c_kernel, lam_init=lam_init),
        grid=(b, 4, SEQ // C_TQ),
        in_specs=[
            pl.BlockSpec((8, LANES), lambda i, h, t: (0, 0)),
            pl.BlockSpec((1, LANES), lambda i, h, t: (0, 0)),
            pl.BlockSpec((1, C_TQ, LANES), lambda i, h, t: (i, t, cq + h)),
            pl.BlockSpec((1, SEQ, LANES), lambda i, h, t: (i, 0, ck + h)),
            pl.BlockSpec((1, SEQ, LANES), lambda i, h, t: (i, 0, cv + h)),
            pl.BlockSpec((1, CTX_LEN, LANES), lambda i, h, t: (i, 0, ckc + h)),
            pl.BlockSpec((1, CTX_LEN, LANES), lambda i, h, t: (i, 0, cvc + h)),
        ],
        out_specs=pl.BlockSpec((1, C_TQ, LANES), lambda i, h, t: (i, t, h)),
        out_shape=jax.ShapeDtypeStruct((b, SEQ, GROUP_WIDTH), BF16),
        compiler_params=_params(("arbitrary",) * 3, 48),
        name="attn_c",
    )(lam_rows, gsub, px, px, px, pc, pc)


D_TQ = 128
D_SPAN = D_TQ + 2 * WINDOW


def _attn_d_kernel(sink_ref, q_ref, k_ref, v_ref, kc_ref, vc_ref, o_ref):
    scale = HEAD_DIM ** -0.5
    kvh = pl.program_id(1)
    kc, vc = kc_ref[0], vc_ref[0]
    rel = (lax.broadcasted_iota(jnp.int32, (D_TQ, D_SPAN), 0)
           - lax.broadcasted_iota(jnp.int32, (D_TQ, D_SPAN), 1))

    def block(i, carry):
        q0 = pl.multiple_of(i * D_TQ, D_TQ)
        k0 = pl.multiple_of(jnp.clip(i * D_TQ - WINDOW, 0, SEQ - D_SPAN), D_TQ)
        kw = k_ref[0, pl.ds(k0, D_SPAN), :]
        vw = v_ref[0, pl.ds(k0, D_SPAN), :]
        valid = jnp.abs(rel + (q0 - k0)) <= WINDOW
        for g in range(2):
            q = q_ref[0, pl.ds(q0, D_TQ), g * LANES:(g + 1) * LANES]
            sw = jnp.where(valid, _dot_t(q, kw) * scale, NEG_INF)
            sc = _dot_t(q, kc) * scale
            sink = sink_ref[kvh * 2 + g]
            m = jnp.maximum(jnp.maximum(jnp.max(sw, axis=-1, keepdims=True),
                                        jnp.max(sc, axis=-1, keepdims=True)), sink)
            ew = jnp.exp(sw - m)
            ec = jnp.exp(sc - m)
            den = (jnp.sum(ew, axis=-1, keepdims=True) + jnp.sum(ec, axis=-1, keepdims=True)
                   + jnp.exp(sink - m))
            o = _dot(ew.astype(BF16), vw) + _dot(ec.astype(BF16), vc)
            o_ref[0, pl.ds(q0, D_TQ), g * LANES:(g + 1) * LANES] = (o / den).astype(BF16)
        return carry

    lax.fori_loop(0, SEQ // D_TQ, block, 0)


def _attn_d_call(px, pc, ctx_tiles, sink):
    b = px.shape[0]
    dq, dk, dv = SEG["dq"], SEG["dk"], SEG["dv"]
    dkc, dvc = _seg_block(ctx_tiles, "dk"), _seg_block(ctx_tiles, "dv")
    return pl.pallas_call(
        _attn_d_kernel,
        grid=(b, 2),
        in_specs=[
            pl.BlockSpec(memory_space=pltpu.SMEM),
            pl.BlockSpec((1, SEQ, 2 * LANES), lambda i, h: (i, 0, dq // 2 + h)),
            pl.BlockSpec((1, SEQ, LANES), lambda i, h: (i, 0, dk + h)),
            pl.BlockSpec((1, SEQ, LANES), lambda i, h: (i, 0, dv + h)),
            pl.BlockSpec((1, CTX_LEN, LANES), lambda i, h: (i, 0, dkc + h)),
            pl.BlockSpec((1, CTX_LEN, LANES), lambda i, h: (i, 0, dvc + h)),
        ],
        out_specs=pl.BlockSpec((1, SEQ, 2 * LANES), lambda i, h: (i, 0, h)),
        out_shape=jax.ShapeDtypeStruct((b, SEQ, GROUP_WIDTH), BF16),
        compiler_params=_params(("arbitrary",) * 2, 32),
        name="attn_d",
    )(sink, px, px, px, pc, pc)


def _softmax_pv(q, k, v, scale, sink=None):
    s = _dot_t(q, k) * scale
    m = jnp.max(s, axis=-1, keepdims=True)
    if sink is not None:
        m = jnp.maximum(m, sink)
    e = jnp.exp(s - m)
    den = jnp.sum(e, axis=-1, keepdims=True)
    if sink is not None:
        den = den + jnp.exp(sink - m)
    return _dot(e.astype(BF16), v) / den


def _ctx_attn_kernel(sink_ref, lam_ref, gsub_ref, p_ref, o_ref, *, lam_init):
    scale = HEAD_DIM ** -0.5

    def blk(name, h):
        c = (SEG[name] + h) * LANES
        return p_ref[0, :, c:c + LANES]

    def put(group, h, o):
        c = group * GROUP_WIDTH + h * LANES
        o_ref[0, :, c:c + LANES] = o.astype(BF16)

    lam = _diff_lambda(lam_ref, lam_init)
    for h in range(4):
        put(0, h, _softmax_pv(blk("aq", h), blk("ak", h // 2), blk("av", h // 2), scale))
        put(1, h, _softmax_pv(blk("bq", h), blk("bk", h), blk("bv", h), scale))
        put(2, h, _diff_head(blk("cq", h), [blk("ck", h)], [blk("cv", h)], lam, gsub_ref[...], lam_init))
        put(3, h, _softmax_pv(blk("dq", h), blk("dk", h // 2), blk("dv", h // 2), scale, sink=sink_ref[h]))


def _ctx_attn_call(pc, sink, lam_rows, gsub, lam_init):
    b = pc.shape[0]
    return pl.pallas_call(
        functools.partial(_ctx_attn_kernel, lam_init=lam_init),
        grid=(b,),
        in_specs=[
            pl.BlockSpec(memory_space=pltpu.SMEM),
            pl.BlockSpec((8, LANES), lambda i: (0, 0)),
            pl.BlockSpec((1, LANES), lambda i: (0, 0)),
            pl.BlockSpec((1, CTX_LEN, PROJ_WIDTH), lambda i: (i, 0, 0)),
        ],
        out_specs=pl.BlockSpec((1, CTX_LEN, D_MODEL), lambda i: (i, 0, 0)),
        out_shape=jax.ShapeDtypeStruct((b, CTX_LEN, D_MODEL), BF16),
        compiler_params=_params(("arbitrary",), 32),
        name="ctx_attn",
    )(sink, lam_rows, gsub, pc)


POST_TM = 512
POST_TH = 512


def _post_kernel(x_ref, oa_ref, ob_ref, oc_ref, od_ref, wout_ref, gate_mix_ref, shift_ref, scale_ref,
                 gate_mlp_ref, g_ref, wup_ref, wdn_ref, gfin_ref, out_ref, x1_ref, h_ref, *, final):
    j = pl.program_id(1)

    @pl.when(j == 0)
    def _():
        mix = functools.reduce(jnp.add, [
            _dot(o[...], wout_ref[n * GROUP_WIDTH:(n + 1) * GROUP_WIDTH, :])
            for n, o in enumerate((oa_ref, ob_ref, oc_ref, od_ref))])
        x1 = x_ref[...] + gate_mix_ref[0] * mix
        x1_ref[...] = x1
        h_ref[...] = (_rms(x1, g_ref[...]) * (1.0 + scale_ref[0]) + shift_ref[0]).astype(BF16)
        out_ref[...] = jnp.zeros_like(out_ref)

    u = _dot(h_ref[...], wup_ref[...])
    act = jnp.square(jnp.maximum(u, 0.0)).astype(BF16)
    out_ref[...] += _dot(act, wdn_ref[...])

    @pl.when(j == pl.num_programs(1) - 1)
    def _():
        y = x1_ref[...] + gate_mlp_ref[0] * out_ref[...]
        if final:
            y = _rms(y, gfin_ref[...])
        out_ref[...] = y


def _post_call(x2d, mix_parts, mix_blocks, w_out, modl, g_mlp, w_up, w_down, g_final, *, mod_row, final):
    t_tok = x2d.shape[0]
    tm, th = POST_TM, POST_TH
    once = pl.Buffered(1)

    def mod_spec(k):
        return pl.BlockSpec((1, 1, D_MODEL), lambda i, j: (mod_row(i) * N_MOD + k, 0, 0))

    def part_spec(blk):
        return pl.BlockSpec((tm, GROUP_WIDTH), lambda i, j: (i, blk), pipeline_mode=once)

    return pl.pallas_call(
        functools.partial(_post_kernel, final=final),
        grid=(t_tok // tm, MLP_HIDDEN // th),
        in_specs=[
            pl.BlockSpec((tm, D_MODEL), lambda i, j: (i, 0), pipeline_mode=once),
            *[part_spec(blk) for blk in mix_blocks],
            pl.BlockSpec((D_MODEL, D_MODEL), lambda i, j: (0, 0), pipeline_mode=once),
            mod_spec(2), mod_spec(3), mod_spec(4), mod_spec(5),
            pl.BlockSpec((1, D_MODEL), lambda i, j: (0, 0)),
            pl.BlockSpec((D_MODEL, th), lambda i, j: (0, j)),
            pl.BlockSpec((th, D_MODEL), lambda i, j: (j, 0)),
            pl.BlockSpec((1, D_MODEL), lambda i, j: (0, 0)),
        ],
        out_specs=pl.BlockSpec((tm, D_MODEL), lambda i, j: (i, 0)),
        out_shape=jax.ShapeDtypeStruct((t_tok, D_MODEL), F32),
        scratch_shapes=[pltpu.VMEM((tm, D_MODEL), F32), pltpu.VMEM((tm, D_MODEL), BF16)],
        compiler_params=_params(("arbitrary", "arbitrary"), 56),
        name="post",
    )(x2d, *mix_parts, w_out, modl, modl, modl, modl, g_mlp, w_up, w_down, g_final)


def _rope_rows(n_tok, dim):
    t = jnp.arange(n_tok)
    row = (t // GRID_W).astype(F32)
    col = (t % GRID_W).astype(F32)
    n_freq = dim // 4
    inv_freq = ROPE_THETA ** (-jnp.arange(n_freq, dtype=F32) / n_freq)
    ang = jnp.concatenate([row[:, None] * inv_freq, col[:, None] * inv_freq], axis=-1)
    cos = jnp.repeat(jnp.cos(ang), 2, axis=-1)
    sin = jnp.repeat(jnp.sin(ang), 2, axis=-1)
    even = (jnp.arange(dim) % 2) == 0
    parts = [cos, jnp.where(even, -sin, 0.0), jnp.where(even, 0.0, sin)]
    return [jnp.tile(p, (1, LANES // dim)) for p in parts]


def _rope_table(n_tok):
    return jnp.concatenate(_rope_rows(n_tok, HEAD_DIM) + _rope_rows(n_tok, DIFF_QK_DIM), axis=-1)


def _identity_rope_table(n_tok):
    one = jnp.ones((n_tok, LANES), F32)
    zero = jnp.zeros((n_tok, LANES), F32)
    return jnp.concatenate([one, zero, zero] * 2, axis=-1)


def _pad_rows(rows, n_rows=8):
    out = jnp.zeros((n_rows, LANES), F32)
    for r, vec in enumerate(rows):
        out = out.at[r, :vec.shape[0]].set(vec.astype(F32))
    return out


def kernel(x, c, ctx, c_ctx, g_mix, g_mlp, w_mod, b_mod, w_in, w_out, gqa_gq, gqa_gk, na_rpb,
           diff_lq1, diff_lk1, diff_lq2, diff_lk2, diff_gsub, swa_sink, w_up, w_down, g_final):
    b, s, d = x.shape
    assert (s, d) == (SEQ, D_MODEL) and ctx.shape == (b, CTX_LEN, D_MODEL) and b + 1 <= MOD_ROWS

    c_rows = jnp.zeros((MOD_ROWS, D_MODEL), F32).at[:b].set(c).at[CTX_MOD_ROW].set(c_ctx)
    mod = _mod_call(c_rows, w_mod, b_mod)

    rope_x = _rope_table(SEQ)
    rope_c = _identity_rope_table(PROJ_TM)
    x_tiles_per_seq = SEQ // PROJ_TM
    x_post_tiles_per_seq = SEQ // POST_TM
    g_final2 = g_final.reshape(1, D_MODEL)

    x2d = x.reshape(b * s, D_MODEL)
    c2d = ctx.reshape(b * CTX_LEN, D_MODEL)
    for l in range(DEPTH):
        with_ctx = l < DEPTH - 1
        lam_init = 0.8 - 0.6 * math.exp(-0.3 * l)
        modl = mod[l].reshape(MOD_ROWS * N_MOD, 1, D_MODEL)
        w_in_l = w_in[l].astype(BF16)
        w_out_l = w_out[l].astype(BF16)
        w_up_l = w_up[l].astype(BF16)
        w_down_l = w_down[l].astype(BF16)
        g_mix_l = g_mix[l].reshape(1, D_MODEL)
        g_mlp_l = g_mlp[l].reshape(1, D_MODEL)
        head_gains = _pad_rows([gqa_gq[l], gqa_gk[l]])
        lam_rows = _pad_rows([diff_lq1[l], diff_lk1[l], diff_lq2[l], diff_lk2[l]])
        gsub = diff_gsub[l].reshape(1, HEAD_DIM)
        ctx_tiles = ALL_TILES if with_ctx else KV_TILES

        px = _proj_call(x2d, modl, g_mix_l, w_in_l, rope_x, head_gains, tiles=ALL_TILES,
                        mod_row=lambda i: i // x_tiles_per_seq, rope_tile=lambda i: i % x_tiles_per_seq)
        pc = _proj_call(c2d, modl, g_mix_l, w_in_l, rope_c, head_gains, tiles=ctx_tiles,
                        mod_row=lambda i: CTX_MOD_ROW, rope_tile=lambda i: 0)
        px = px.reshape(b, SEQ, PROJ_WIDTH)
        pc = pc.reshape(b, CTX_LEN, len(ctx_tiles) * COL_TILE)

        oa = _attn_a_call(px, pc, ctx_tiles)
        ob = _attn_b_call(px, pc, ctx_tiles, _na_bias_table(na_rpb[l]))
        oc = _attn_c_call(px, pc, ctx_tiles, lam_rows, gsub, lam_init)
        od = _attn_d_call(px, pc, ctx_tiles, swa_sink[l])
        parts = [o.reshape(b * s, GROUP_WIDTH) for o in (oa, ob, oc, od)]
        if with_ctx:
            o_ctx = _ctx_attn_call(pc, swa_sink[l], lam_rows, gsub, lam_init).reshape(b * CTX_LEN, D_MODEL)

        x2d = _post_call(x2d, parts, (0, 0, 0, 0), w_out_l, modl, g_mlp_l, w_up_l, w_down_l, g_final2,
                         mod_row=lambda i: i // x_post_tiles_per_seq, final=not with_ctx)
        if with_ctx:
            c2d = _post_call(c2d, [o_ctx] * 4, (0, 1, 2, 3), w_out_l, modl, g_mlp_l, w_up_l, w_down_l,
                             g_final2, mod_row=lambda i: CTX_MOD_ROW, final=False)
    return x2d.reshape(b, s, D_MODEL)
```

```python
import functools
import math

import jax
import jax.numpy as jnp
from jax import lax
from jax.experimental import pallas as pl
from jax.experimental.pallas import tpu as pltpu

D_MODEL = 2048
SEQ = 2048
DEPTH = 2
GRID_W = 64
CTX_LEN = 256
HEAD_DIM = 128
GROUP_WIDTH = 512
WIN_ROWS = 8
WIN_COLS = 16
DIFF_QK_DIM = 64
WINDOW = 128
MLP_HIDDEN = 4 * D_MODEL
N_MOD = 6
ROPE_THETA = 10000.0
NORM_EPS = 1e-6
NEG_INF = -1e30
PROJ_WIDTH = 5120
LOG2E = math.log2(math.e)

LANES = 128
COL_TILE = 512
MOD_ROWS = 16
CTX_MOD_ROW = 8

F32 = jnp.float32
BF16 = jnp.bfloat16

SEG = dict(aq=0, ak=4, av=6, bq=8, bk=12, bv=16, cq=20, ck=24, cv=28, dq=32, dk=36, dv=38)
ALL_TILES = tuple(range(PROJ_WIDTH // COL_TILE))
KV_TILES = (1, 3, 4, 6, 7, 9)
_ROPE_H, _ROPE_D = 0, 1
BLOCK_KIND = {}
for _b in range(PROJ_WIDTH // LANES):
    if _b < 4:
        BLOCK_KIND[_b] = (0, _ROPE_H)
    elif _b < 6:
        BLOCK_KIND[_b] = (1, _ROPE_H)
    elif 20 <= _b < 28:
        BLOCK_KIND[_b] = (None, _ROPE_D)
    elif 32 <= _b < 38:
        BLOCK_KIND[_b] = (None, _ROPE_H)
    else:
        BLOCK_KIND[_b] = (None, None)


def _seg_block(tiles, name):
    b = SEG[name]
    t, r = divmod(b, COL_TILE // LANES)
    return tiles.index(t) * (COL_TILE // LANES) + r


def _params(dims, vmem_mb):
    return pltpu.CompilerParams(dimension_semantics=dims, vmem_limit_bytes=vmem_mb * 1024 * 1024)


def _rms(x, g):
    ms = jnp.mean(x * x, axis=-1, keepdims=True)
    return x * lax.rsqrt(ms + NORM_EPS) * g


def _dot(a, b):
    return jnp.dot(a, b, preferred_element_type=F32)


def _dot_t(a, b):
    return lax.dot_general(a, b, (((1,), (1,)), ((), ())), preferred_element_type=F32)


def _dot_tn(a, b):
    return lax.dot_general(a, b, (((0,), (0,)), ((), ())), preferred_element_type=F32)


def _colmax(parts):
    return functools.reduce(jnp.maximum, [jnp.max(p, axis=0, keepdims=True) for p in parts])


def _colsum(parts):
    return functools.reduce(jnp.add, [jnp.sum(p, axis=0, keepdims=True) for p in parts])


def _rowmax(parts):
    return functools.reduce(jnp.maximum, [jnp.max(p, axis=-1, keepdims=True) for p in parts])


def _rowsum(parts):
    return functools.reduce(jnp.add, [jnp.sum(p, axis=-1, keepdims=True) for p in parts])


MOD_TN = 1024


def _mod_kernel(c_ref, w_ref, b_ref, o_ref):
    c = c_ref[...]
    cond = c * (1.0 / (1.0 + jnp.exp(-c)))
    o_ref[0] = _dot(cond.astype(BF16), w_ref[0].astype(BF16)) + b_ref[0]


def _mod_call(c_rows, w_mod, b_mod):
    n = N_MOD * D_MODEL
    return pl.pallas_call(
        _mod_kernel,
        grid=(DEPTH, n // MOD_TN),
        in_specs=[
            pl.BlockSpec((MOD_ROWS, D_MODEL), lambda l, j: (0, 0)),
            pl.BlockSpec((1, D_MODEL, MOD_TN), lambda l, j: (l, 0, j)),
            pl.BlockSpec((1, 1, MOD_TN), lambda l, j: (l, 0, j)),
        ],
        out_specs=pl.BlockSpec((1, MOD_ROWS, MOD_TN), lambda l, j: (l, 0, j)),
        out_shape=jax.ShapeDtypeStruct((DEPTH, MOD_ROWS, n), F32),
        compiler_params=_params(("arbitrary", "arbitrary"), 40),
        name="mod",
    )(c_rows, w_mod, b_mod.reshape(DEPTH, 1, n))


PROJ_TM = 512


def _proj_kernel(x_ref, shift_ref, scale_ref, g_ref, w_ref, rope_ref, hg_ref, o_ref, *, tiles):
    h = _rms(x_ref[...], g_ref[...]) * (1.0 + scale_ref[0]) + shift_ref[0]
    hb = h.astype(BF16)
    per = COL_TILE // LANES
    for n, t in enumerate(tiles):
        y = _dot(hb, w_ref[:, t * COL_TILE:(t + 1) * COL_TILE])
        for j in range(per):
            gain_row, rope = BLOCK_KIND[t * per + j]
            yj = y[:, j * LANES:(j + 1) * LANES]
            if gain_row is not None:
                yj = _rms(yj, hg_ref[gain_row:gain_row + 1, :])
            if rope is not None:
                base = rope * 3 * LANES
                cos = rope_ref[:, base:base + LANES]
                sin_next = rope_ref[:, base + LANES:base + 2 * LANES]
                sin_prev = rope_ref[:, base + 2 * LANES:base + 3 * LANES]
                yj = (yj * cos + pltpu.roll(yj, LANES - 1, 1) * sin_next
                      + pltpu.roll(yj, 1, 1) * sin_prev)
            o_ref[:, (n * per + j) * LANES:(n * per + j + 1) * LANES] = yj.astype(BF16)


def _proj_call(x2d, modl, g, w_in, layer, rope, head_gains, *, tiles, mod_row, rope_tile):
    t_tok = x2d.shape[0]
    tm = PROJ_TM
    wout = len(tiles) * COL_TILE
    return pl.pallas_call(
        functools.partial(_proj_kernel, tiles=tiles),
        grid=(t_tok // tm,),
        in_specs=[
            pl.BlockSpec((tm, D_MODEL), lambda i: (i, 0)),
            pl.BlockSpec((1, 1, D_MODEL), lambda i: (mod_row(i) * N_MOD + 0, 0, 0)),
            pl.BlockSpec((1, 1, D_MODEL), lambda i: (mod_row(i) * N_MOD + 1, 0, 0)),
            pl.BlockSpec((1, D_MODEL), lambda i: (0, 0)),
            pl.BlockSpec((None, D_MODEL, PROJ_WIDTH), lambda i: (layer, 0, 0), pipeline_mode=pl.Buffered(1)),
            pl.BlockSpec((tm, 6 * LANES), lambda i: (rope_tile(i), 0)),
            pl.BlockSpec((8, LANES), lambda i: (0, 0)),
        ],
        out_specs=pl.BlockSpec((tm, wout), lambda i: (i, 0)),
        out_shape=jax.ShapeDtypeStruct((t_tok, wout), BF16),
        compiler_params=_params(("arbitrary",), 56),
        name="proj",
    )(x2d, modl, modl, g, w_in, rope, head_gains)


A_TQ = 512


def _attn_a_kernel(q_ref, k_ref, v_ref, kc_ref, vc_ref, o_ref):
    c2 = HEAD_DIM ** -0.5 * LOG2E
    k, v, kc, vc = k_ref[0], v_ref[0], kc_ref[0], vc_ref[0]
    for g in range(2):
        q = q_ref[0, :, g * LANES:(g + 1) * LANES]
        sx = _dot_t(k, q)
        sc = _dot_t(kc, q)
        m = _colmax([sx, sc])
        ex = jnp.exp2((sx - m) * c2)
        ec = jnp.exp2((sc - m) * c2)
        den = _colsum([ex, ec])
        o_t = _dot_tn(v, ex.astype(BF16)) + _dot_tn(vc, ec.astype(BF16))
        o_ref[0, :, g * LANES:(g + 1) * LANES] = (o_t / den).T.astype(BF16)


def _attn_a_call(px, pc, ctx_tiles):
    b = px.shape[0]
    ak, av = SEG["ak"], SEG["av"]
    akc, avc = _seg_block(ctx_tiles, "ak"), _seg_block(ctx_tiles, "av")
    return pl.pallas_call(
        _attn_a_kernel,
        grid=(b, 2, SEQ // A_TQ),
        in_specs=[
            pl.BlockSpec((1, A_TQ, 2 * LANES), lambda i, h, t: (i, t, h)),
            pl.BlockSpec((1, SEQ, LANES), lambda i, h, t: (i, 0, ak + h)),
            pl.BlockSpec((1, SEQ, LANES), lambda i, h, t: (i, 0, av + h)),
            pl.BlockSpec((1, CTX_LEN, LANES), lambda i, h, t: (i, 0, akc + h)),
            pl.BlockSpec((1, CTX_LEN, LANES), lambda i, h, t: (i, 0, avc + h)),
        ],
        out_specs=pl.BlockSpec((1, A_TQ, 2 * LANES), lambda i, h, t: (i, t, h)),
        out_shape=jax.ShapeDtypeStruct((b, SEQ, GROUP_WIDTH), BF16),
        compiler_params=_params(("arbitrary",) * 3, 48),
        name="attn_a",
    )(px, px, px, pc, pc)


NB_ROWS = SEQ // GRID_W
NB_QROWS = 4
NB_KROWS = NB_QROWS + WIN_ROWS - 1
NB_BLOCKS = NB_ROWS // NB_QROWS
NB_TQ = NB_QROWS * GRID_W
NB_KEYS = NB_KROWS * GRID_W


def _nb_window_start(j):
    return min(max(j * NB_QROWS - WIN_ROWS // 2, 0), NB_ROWS - NB_KROWS)


def _nb_class(j):
    return 0 if j == 0 else (2 if j == NB_BLOCKS - 1 else 1)


def _na_bias_table(rpb):
    n_head, n_dr, n_dc = rpb.shape
    period = 2 * GRID_W
    fill = jnp.full((n_head, n_dr, period - n_dc), NEG_INF, F32)
    ext = jnp.concatenate([rpb.astype(F32)[..., WIN_COLS - 1:], fill, rpb.astype(F32)[..., :WIN_COLS - 1]], axis=-1)
    flat = jnp.tile(ext, (1, 1, GRID_W))[..., :GRID_W * (period - 1)]
    toep = flat.reshape(n_head, n_dr, GRID_W, period - 1)[..., :GRID_W]
    c = jnp.arange(GRID_W)
    c_start = jnp.clip(c - WIN_COLS // 2, 0, GRID_W - WIN_COLS)
    kc = jnp.arange(GRID_W)
    in_win = (kc[None, :] >= c_start[:, None]) & (kc[None, :] < c_start[:, None] + WIN_COLS)
    toep = jnp.where(in_win, toep * LOG2E, NEG_INF)
    masked = jnp.full((n_head, GRID_W, GRID_W), NEG_INF, F32)
    classes = []
    for j in (0, 1, NB_BLOCKS - 1):
        q_rows = []
        for rq in range(NB_QROWS):
            r = j * NB_QROWS + rq
            r_start = min(max(r - WIN_ROWS // 2, 0), NB_ROWS - WIN_ROWS)
            k_rows = []
            for rk in range(NB_KROWS):
                r_key = _nb_window_start(j) + rk
                visible = r_start <= r_key < r_start + WIN_ROWS
                k_rows.append(toep[:, r_key - r + WIN_ROWS - 1] if visible else masked)
            q_rows.append(jnp.stack(k_rows, axis=2))
        classes.append(jnp.stack(q_rows, axis=1))
    return jnp.stack(classes, axis=1).reshape(n_head, 3, NB_TQ, NB_KEYS)


def _attn_b_kernel(q_ref, k_ref, v_ref, kc_ref, vc_ref, bias_ref, o_ref):
    c2 = HEAD_DIM ** -0.5 * LOG2E
    kc, vc = kc_ref[0], vc_ref[0]
    for j in range(NB_BLOCKS):
        q0 = j * NB_TQ
        k0 = _nb_window_start(j) * GRID_W
        q = q_ref[0, q0:q0 + NB_TQ, :]
        kw = k_ref[0, k0:k0 + NB_KEYS, :]
        vw = v_ref[0, k0:k0 + NB_KEYS, :]
        sn = _dot_t(q, kw) * c2 + bias_ref[0, _nb_class(j)]
        sc = _dot_t(q, kc) * c2
        m = _rowmax([sn, sc])
        en = jnp.exp2(sn - m)
        ec = jnp.exp2(sc - m)
        den = _rowsum([en, ec])
        o = _dot(en.astype(BF16), vw) + _dot(ec.astype(BF16), vc)
        o_ref[0, q0:q0 + NB_TQ, :] = (o / den).astype(BF16)


def _attn_b_call(px, pc, ctx_tiles, bias):
    b = px.shape[0]
    bq, bk, bv = SEG["bq"], SEG["bk"], SEG["bv"]
    bkc, bvc = _seg_block(ctx_tiles, "bk"), _seg_block(ctx_tiles, "bv")
    return pl.pallas_call(
        _attn_b_kernel,
        grid=(b, 4),
        in_specs=[
            pl.BlockSpec((1, SEQ, LANES), lambda i, h: (i, 0, bq + h)),
            pl.BlockSpec((1, SEQ, LANES), lambda i, h: (i, 0, bk + h)),
            pl.BlockSpec((1, SEQ, LANES), lambda i, h: (i, 0, bv + h)),
            pl.BlockSpec((1, CTX_LEN, LANES), lambda i, h: (i, 0, bkc + h)),
            pl.BlockSpec((1, CTX_LEN, LANES), lambda i, h: (i, 0, bvc + h)),
            pl.BlockSpec((1, 3, NB_TQ, NB_KEYS), lambda i, h: (h, 0, 0, 0)),
        ],
        out_specs=pl.BlockSpec((1, SEQ, LANES), lambda i, h: (i, 0, h)),
        out_shape=jax.ShapeDtypeStruct((b, SEQ, GROUP_WIDTH), BF16),
        compiler_params=_params(("arbitrary",) * 2, 32),
        name="attn_b",
    )(px, px, px, pc, pc, bias)


C_TQ = 512


def _diff_lambda(lam_ref, lam_init):
    a = jnp.sum(lam_ref[0:1, :] * lam_ref[1:2, :], axis=-1, keepdims=True)
    b = jnp.sum(lam_ref[2:3, :] * lam_ref[3:4, :], axis=-1, keepdims=True)
    return jnp.exp(a) - jnp.exp(b) + lam_init


def _diff_head(q, keys, vals, lam, gsub, lam_init):
    c2 = DIFF_QK_DIM ** -0.5 * LOG2E
    lane = lax.broadcasted_iota(jnp.int32, q.shape, 1)
    zero = jnp.zeros_like(q)
    maps = []
    for sub in range(2):
        qs = jnp.where((lane < DIFF_QK_DIM) == (sub == 0), q, zero)
        s = [_dot_t(k, qs) for k in keys]
        m = _colmax(s)
        e = [jnp.exp2((x - m) * c2) for x in s]
        maps.append((e, _colsum(e)))
    (e1, den1), (e2, den2) = maps
    ratio = lam * den1 / den2
    o_t = functools.reduce(jnp.add, [
        _dot_tn(v, (a - ratio * b).astype(BF16)) for a, b, v in zip(e1, e2, vals)]) / den1
    return _rms(o_t.T, gsub) * (1.0 - lam_init)


def _attn_c_kernel(lam_ref, gsub_ref, q_ref, k_ref, v_ref, kc_ref, vc_ref, o_ref, *, lam_init):
    lam = _diff_lambda(lam_ref, lam_init)
    o = _diff_head(q_ref[0], [k_ref[0], kc_ref[0]], [v_ref[0], vc_ref[0]], lam, gsub_ref[...], lam_init)
    o_ref[0] = o.astype(BF16)


def _attn_c_call(px, pc, ctx_tiles, lam_rows, gsub, lam_init):
    b = px.shape[0]
    cq, ck, cv = SEG["cq"], SEG["ck"], SEG["cv"]
    ckc, cvc = _seg_block(ctx_tiles, "ck"), _seg_block(ctx_tiles, "cv")
    return pl.pallas_call(
        functools.partial(_attn_c_kernel, lam_init=lam_init),
        grid=(b, 4, SEQ // C_TQ),
        in_specs=[
            pl.BlockSpec((8, LANES), lambda i, h, t: (0, 0)),
            pl.BlockSpec((1, LANES), lambda i, h, t: (0, 0)),
            pl.BlockSpec((1, C_TQ, LANES), lambda i, h, t: (i, t, cq + h)),
            pl.BlockSpec((1, SEQ, LANES), lambda i, h, t: (i, 0, ck + h)),
            pl.BlockSpec((1, SEQ, LANES), lambda i, h, t: (i, 0, cv + h)),
            pl.BlockSpec((1, CTX_LEN, LANES), lambda i, h, t: (i, 0, ckc + h)),
            pl.BlockSpec((1, CTX_LEN, LANES), lambda i, h, t: (i, 0, cvc + h)),
        ],
        out_specs=pl.BlockSpec((1, C_TQ, LANES), lambda i, h, t: (i, t, h)),
        out_shape=jax.ShapeDtypeStruct((b, SEQ, GROUP_WIDTH), BF16),
        compiler_params=_params(("arbitrary",) * 3, 48),
        name="attn_c",
    )(lam_rows, gsub, px, px, px, pc, pc)


D_TQ = 256
D_SPAN = D_TQ + 2 * WINDOW


def _attn_d_kernel(sink_ref, q_ref, k_ref, v_ref, kc_ref, vc_ref, o_ref):
    c2 = HEAD_DIM ** -0.5 * LOG2E
    kvh = pl.program_id(1)
    kc, vc = kc_ref[0], vc_ref[0]
    row = lax.broadcasted_iota(jnp.int32, (2 * D_TQ, D_SPAN), 0)
    rel = (row & (D_TQ - 1)) - lax.broadcasted_iota(jnp.int32, (2 * D_TQ, D_SPAN), 1)
    head_row = lax.broadcasted_iota(jnp.int32, (2 * D_TQ, 1), 0)
    sink = jnp.where(head_row < D_TQ, sink_ref[kvh * 2], sink_ref[kvh * 2 + 1]) * LOG2E
    for i in range(SEQ // D_TQ):
        q0 = i * D_TQ
        k0 = min(max(q0 - WINDOW, 0), SEQ - D_SPAN)
        kw = k_ref[0, k0:k0 + D_SPAN, :]
        vw = v_ref[0, k0:k0 + D_SPAN, :]
        valid = jnp.abs(rel + (q0 - k0)) <= WINDOW
        q = jnp.concatenate([q_ref[0, q0:q0 + D_TQ, :LANES], q_ref[0, q0:q0 + D_TQ, LANES:]], axis=0)
        sw = jnp.where(valid, _dot_t(q, kw) * c2, NEG_INF)
        sc = _dot_t(q, kc) * c2
        m = jnp.maximum(_rowmax([sw, sc]), sink)
        ew = jnp.exp2(sw - m)
        ec = jnp.exp2(sc - m)
        den = _rowsum([ew, ec]) + jnp.exp2(sink - m)
        o = ((_dot(ew.astype(BF16), vw) + _dot(ec.astype(BF16), vc)) / den).astype(BF16)
        o_ref[0, q0:q0 + D_TQ, :LANES] = o[:D_TQ]
        o_ref[0, q0:q0 + D_TQ, LANES:] = o[D_TQ:]


def _attn_d_call(px, pc, ctx_tiles, sink):
    b = px.shape[0]
    dq, dk, dv = SEG["dq"], SEG["dk"], SEG["dv"]
    dkc, dvc = _seg_block(ctx_tiles, "dk"), _seg_block(ctx_tiles, "dv")
    return pl.pallas_call(
        _attn_d_kernel,
        grid=(b, 2),
        in_specs=[
            pl.BlockSpec(memory_space=pltpu.SMEM),
            pl.BlockSpec((1, SEQ, 2 * LANES), lambda i, h: (i, 0, dq // 2 + h)),
            pl.BlockSpec((1, SEQ, LANES), lambda i, h: (i, 0, dk + h)),
            pl.BlockSpec((1, SEQ, LANES), lambda i, h: (i, 0, dv + h)),
            pl.BlockSpec((1, CTX_LEN, LANES), lambda i, h: (i, 0, dkc + h)),
            pl.BlockSpec((1, CTX_LEN, LANES), lambda i, h: (i, 0, dvc + h)),
        ],
        out_specs=pl.BlockSpec((1, SEQ, 2 * LANES), lambda i, h: (i, 0, h)),
        out_shape=jax.ShapeDtypeStruct((b, SEQ, GROUP_WIDTH), BF16),
        compiler_params=_params(("arbitrary",) * 2, 32),
        name="attn_d",
    )(sink, px, px, px, pc, pc)


def _softmax_pv(q, k, v, c2, sink=None):
    s = _dot_t(q, k) * c2
    m = jnp.max(s, axis=-1, keepdims=True)
    if sink is not None:
        m = jnp.maximum(m, sink)
    e = jnp.exp2(s - m)
    den = jnp.sum(e, axis=-1, keepdims=True)
    if sink is not None:
        den = den + jnp.exp2(sink - m)
    return _dot(e.astype(BF16), v) / den


def _ctx_attn_kernel(sink_ref, lam_ref, gsub_ref, p_ref, o_ref, *, lam_init):
    c2 = HEAD_DIM ** -0.5 * LOG2E

    def blk(name, h):
        c = (SEG[name] + h) * LANES
        return p_ref[0, :, c:c + LANES]

    def put(group, h, o):
        c = group * GROUP_WIDTH + h * LANES
        o_ref[0, :, c:c + LANES] = o.astype(BF16)

    lam = _diff_lambda(lam_ref, lam_init)
    for h in range(4):
        put(0, h, _softmax_pv(blk("aq", h), blk("ak", h // 2), blk("av", h // 2), c2))
        put(1, h, _softmax_pv(blk("bq", h), blk("bk", h), blk("bv", h), c2))
        put(2, h, _diff_head(blk("cq", h), [blk("ck", h)], [blk("cv", h)], lam, gsub_ref[...], lam_init))
        put(3, h, _softmax_pv(blk("dq", h), blk("dk", h // 2), blk("dv", h // 2), c2,
                              sink=sink_ref[h] * LOG2E))


def _ctx_attn_call(pc, sink, lam_rows, gsub, lam_init):
    b = pc.shape[0]
    return pl.pallas_call(
        functools.partial(_ctx_attn_kernel, lam_init=lam_init),
        grid=(b,),
        in_specs=[
            pl.BlockSpec(memory_space=pltpu.SMEM),
            pl.BlockSpec((8, LANES), lambda i: (0, 0)),
            pl.BlockSpec((1, LANES), lambda i: (0, 0)),
            pl.BlockSpec((1, CTX_LEN, PROJ_WIDTH), lambda i: (i, 0, 0)),
        ],
        out_specs=pl.BlockSpec((1, CTX_LEN, D_MODEL), lambda i: (i, 0, 0)),
        out_shape=jax.ShapeDtypeStruct((b, CTX_LEN, D_MODEL), BF16),
        compiler_params=_params(("arbitrary",), 32),
        name="ctx_attn",
    )(sink, lam_rows, gsub, pc)


POST_TM = 512
POST_TH = 1024


def _post_kernel(x_ref, oa_ref, ob_ref, oc_ref, od_ref, wout_ref, gate_mix_ref, shift_ref, scale_ref,
                 gate_mlp_ref, g_ref, wup_ref, wdn_ref, gfin_ref, out_ref, x1_ref, h_ref, *, final):
    j = pl.program_id(1)

    @pl.when(j == 0)
    def _():
        mix = functools.reduce(jnp.add, [
            _dot(o[...], wout_ref[n * GROUP_WIDTH:(n + 1) * GROUP_WIDTH, :])
            for n, o in enumerate((oa_ref, ob_ref, oc_ref, od_ref))])
        x1 = x_ref[...] + gate_mix_ref[0] * mix
        x1_ref[...] = x1
        h_ref[...] = (_rms(x1, g_ref[...]) * (1.0 + scale_ref[0]) + shift_ref[0]).astype(BF16)
        out_ref[...] = jnp.zeros_like(out_ref)

    u = _dot(h_ref[...], wup_ref[...])
    act = jnp.square(jnp.maximum(u, 0.0)).astype(BF16)
    out_ref[...] += _dot(act, wdn_ref[...])

    @pl.when(j == pl.num_programs(1) - 1)
    def _():
        y = x1_ref[...] + gate_mlp_ref[0] * out_ref[...]
        if final:
            y = _rms(y, gfin_ref[...])
        out_ref[...] = y


def _post_call(x2d, mix_parts, mix_blocks, w_out, modl, g_mlp, w_up_tiled, w_down, layer, g_final, *,
               mod_row, final):
    t_tok = x2d.shape[0]
    tm, th = POST_TM, POST_TH
    once = pl.Buffered(1)

    def mod_spec(k):
        return pl.BlockSpec((1, 1, D_MODEL), lambda i, j: (mod_row(i) * N_MOD + k, 0, 0))

    def part_spec(blk):
        return pl.BlockSpec((tm, GROUP_WIDTH), lambda i, j: (i, blk), pipeline_mode=once)

    return pl.pallas_call(
        functools.partial(_post_kernel, final=final),
        grid=(t_tok // tm, MLP_HIDDEN // th),
        in_specs=[
            pl.BlockSpec((tm, D_MODEL), lambda i, j: (i, 0), pipeline_mode=once),
            *[part_spec(blk) for blk in mix_blocks],
            pl.BlockSpec((None, D_MODEL, D_MODEL), lambda i, j: (layer, 0, 0), pipeline_mode=once),
            mod_spec(2), mod_spec(3), mod_spec(4), mod_spec(5),
            pl.BlockSpec((1, D_MODEL), lambda i, j: (0, 0)),
            pl.BlockSpec((None, None, D_MODEL, th), lambda i, j: (layer, j, 0, 0)),
            pl.BlockSpec((None, th, D_MODEL), lambda i, j: (layer, j, 0)),
            pl.BlockSpec((1, D_MODEL), lambda i, j: (0, 0)),
        ],
        out_specs=pl.BlockSpec((tm, D_MODEL), lambda i, j: (i, 0)),
        out_shape=jax.ShapeDtypeStruct((t_tok, D_MODEL), F32),
        scratch_shapes=[pltpu.VMEM((tm, D_MODEL), F32), pltpu.VMEM((tm, D_MODEL), BF16)],
        compiler_params=_params(("arbitrary", "arbitrary"), 58),
        name="post",
    )(x2d, *mix_parts, w_out, modl, modl, modl, modl, g_mlp, w_up_tiled, w_down, g_final)


def _rope_rows(n_tok, dim):
    t = jnp.arange(n_tok)
    row = (t // GRID_W).astype(F32)
    col = (t % GRID_W).astype(F32)
    n_freq = dim // 4
    inv_freq = ROPE_THETA ** (-jnp.arange(n_freq, dtype=F32) / n_freq)
    ang = jnp.concatenate([row[:, None] * inv_freq, col[:, None] * inv_freq], axis=-1)
    cos = jnp.repeat(jnp.cos(ang), 2, axis=-1)
    sin = jnp.repeat(jnp.sin(ang), 2, axis=-1)
    even = (jnp.arange(dim) % 2) == 0
    parts = [cos, jnp.where(even, -sin, 0.0), jnp.where(even, 0.0, sin)]
    return [jnp.tile(p, (1, LANES // dim)) for p in parts]


def _rope_table(n_tok):
    return jnp.concatenate(_rope_rows(n_tok, HEAD_DIM) + _rope_rows(n_tok, DIFF_QK_DIM), axis=-1)


def _identity_rope_table(n_tok):
    one = jnp.ones((n_tok, LANES), F32)
    zero = jnp.zeros((n_tok, LANES), F32)
    return jnp.concatenate([one, zero, zero] * 2, axis=-1)


def _pad_rows(rows, n_rows=8):
    out = jnp.zeros((n_rows, LANES), F32)
    for r, vec in enumerate(rows):
        out = out.at[r, :vec.shape[0]].set(vec.astype(F32))
    return out


def kernel(x, c, ctx, c_ctx, g_mix, g_mlp, w_mod, b_mod, w_in, w_out, gqa_gq, gqa_gk, na_rpb,
           diff_lq1, diff_lk1, diff_lq2, diff_lk2, diff_gsub, swa_sink, w_up, w_down, g_final):
    b, s, d = x.shape
    assert (s, d) == (SEQ, D_MODEL) and ctx.shape == (b, CTX_LEN, D_MODEL) and b + 1 <= MOD_ROWS

    c_rows = jnp.zeros((MOD_ROWS, D_MODEL), F32).at[:b].set(c).at[CTX_MOD_ROW].set(c_ctx)
    mod = _mod_call(c_rows, w_mod, b_mod)

    w_in_b = w_in.astype(BF16)
    w_out_b = w_out.astype(BF16)
    w_up_b = (w_up.astype(BF16).reshape(DEPTH, D_MODEL, MLP_HIDDEN // POST_TH, POST_TH)
              .transpose(0, 2, 1, 3))
    w_down_b = w_down.astype(BF16)

    rope_x = _rope_table(SEQ)
    rope_c = _identity_rope_table(PROJ_TM)
    x_tiles_per_seq = SEQ // PROJ_TM
    x_post_tiles_per_seq = SEQ // POST_TM
    g_final2 = g_final.reshape(1, D_MODEL)

    x2d = x.reshape(b * s, D_MODEL)
    c2d = ctx.reshape(b * CTX_LEN, D_MODEL)
    for l in range(DEPTH):
        with_ctx = l < DEPTH - 1
        lam_init = 0.8 - 0.6 * math.exp(-0.3 * l)
        modl = mod[l].reshape(MOD_ROWS * N_MOD, 1, D_MODEL)
        g_mix_l = g_mix[l].reshape(1, D_MODEL)
        g_mlp_l = g_mlp[l].reshape(1, D_MODEL)
        head_gains = _pad_rows([gqa_gq[l], gqa_gk[l]])
        lam_rows = _pad_rows([diff_lq1[l], diff_lk1[l], diff_lq2[l], diff_lk2[l]])
        gsub = diff_gsub[l].reshape(1, HEAD_DIM)
        ctx_tiles = ALL_TILES if with_ctx else KV_TILES

        px = _proj_call(x2d, modl, g_mix_l, w_in_b, l, rope_x, head_gains, tiles=ALL_TILES,
                        mod_row=lambda i: i // x_tiles_per_seq, rope_tile=lambda i: i % x_tiles_per_seq)
        pc = _proj_call(c2d, modl, g_mix_l, w_in_b, l, rope_c, head_gains, tiles=ctx_tiles,
                        mod_row=lambda i: CTX_MOD_ROW, rope_tile=lambda i: 0)
        px = px.reshape(b, SEQ, PROJ_WIDTH)
        pc = pc.reshape(b, CTX_LEN, len(ctx_tiles) * COL_TILE)

        oa = _attn_a_call(px, pc, ctx_tiles)
        ob = _attn_b_call(px, pc, ctx_tiles, _na_bias_table(na_rpb[l]))
        oc = _attn_c_call(px, pc, ctx_tiles, lam_rows, gsub, lam_init)
        od = _attn_d_call(px, pc, ctx_tiles, swa_sink[l])
        parts = [o.reshape(b * s, GROUP_WIDTH) for o in (oa, ob, oc, od)]
        if with_ctx:
            o_ctx = _ctx_attn_call(pc, swa_sink[l], lam_rows, gsub, lam_init).reshape(b * CTX_LEN, D_MODEL)

        x2d = _post_call(x2d, parts, (0, 0, 0, 0), w_out_b, modl, g_mlp_l, w_up_b, w_down_b, l, g_final2,
                         mod_row=lambda i: i // x_post_tiles_per_seq, final=not with_ctx)
        if with_ctx:
            c2d = _post_call(c2d, [o_ctx] * 4, (0, 1, 2, 3), w_out_b, modl, g_mlp_l, w_up_b, w_down_b, l,
                             g_final2, mod_row=lambda i: CTX_MOD_ROW, final=False)
    return x2d.reshape(b, s, D_MODEL)
```

```python
import functools
import math

import jax
import jax.numpy as jnp
from jax import lax
from jax.experimental import pallas as pl
from jax.experimental.pallas import tpu as pltpu

D_MODEL = 2048
SEQ = 2048
DEPTH = 2
GRID_W = 64
CTX_LEN = 256
HEAD_DIM = 128
GROUP_WIDTH = 512
WIN_ROWS = 8
WIN_COLS = 16
DIFF_QK_DIM = 64
WINDOW = 128
MLP_HIDDEN = 4 * D_MODEL
N_MOD = 6
ROPE_THETA = 10000.0
NORM_EPS = 1e-6
NEG_INF = -1e30
PROJ_WIDTH = 5120
LOG2E = math.log2(math.e)

LANES = 128
COL_TILE = 512
MOD_ROWS = 16
CTX_MOD_ROW = 8

F32 = jnp.float32
BF16 = jnp.bfloat16

SEG = dict(aq=0, ak=4, av=6, bq=8, bk=12, bv=16, cq=20, ck=24, cv=28, dq=32, dk=36, dv=38)
ALL_TILES = tuple(range(PROJ_WIDTH // COL_TILE))
KV_TILES = (1, 3, 4, 6, 7, 9)
Q_SCALE_H = HEAD_DIM ** -0.5 * LOG2E
Q_SCALE_D = DIFF_QK_DIM ** -0.5 * LOG2E
_ROPE_H, _ROPE_D = 0, 1
BLOCK_KIND = {}
for _b in range(PROJ_WIDTH // LANES):
    if _b < 4:
        BLOCK_KIND[_b] = (0, _ROPE_H, Q_SCALE_H)
    elif _b < 6:
        BLOCK_KIND[_b] = (1, _ROPE_H, None)
    elif 8 <= _b < 12:
        BLOCK_KIND[_b] = (None, None, Q_SCALE_H)
    elif 20 <= _b < 24:
        BLOCK_KIND[_b] = (None, _ROPE_D, Q_SCALE_D)
    elif 24 <= _b < 28:
        BLOCK_KIND[_b] = (None, _ROPE_D, None)
    elif 32 <= _b < 36:
        BLOCK_KIND[_b] = (None, _ROPE_H, Q_SCALE_H)
    elif 36 <= _b < 38:
        BLOCK_KIND[_b] = (None, _ROPE_H, None)
    else:
        BLOCK_KIND[_b] = (None, None, None)


def _seg_block(tiles, name):
    b = SEG[name]
    t, r = divmod(b, COL_TILE // LANES)
    return tiles.index(t) * (COL_TILE // LANES) + r


def _params(dims, vmem_mb, flags=None):
    return pltpu.CompilerParams(dimension_semantics=dims, vmem_limit_bytes=vmem_mb * 1024 * 1024, flags=flags)


def _rms(x, g):
    ms = jnp.mean(x * x, axis=-1, keepdims=True)
    return x * lax.rsqrt(ms + NORM_EPS) * g


def _dot(a, b):
    return jnp.dot(a, b, preferred_element_type=F32)


def _dot_t(a, b):
    return lax.dot_general(a, b, (((1,), (1,)), ((), ())), preferred_element_type=F32)


def _dot_tn(a, b):
    return lax.dot_general(a, b, (((0,), (0,)), ((), ())), preferred_element_type=F32)


def _colmax(parts):
    return functools.reduce(jnp.maximum, [jnp.max(p, axis=0, keepdims=True) for p in parts])


def _colsum(parts):
    return functools.reduce(jnp.add, [jnp.sum(p, axis=0, keepdims=True) for p in parts])


def _rowmax(parts):
    return functools.reduce(jnp.maximum, [jnp.max(p, axis=-1, keepdims=True) for p in parts])


def _rowsum(parts):
    return functools.reduce(jnp.add, [jnp.sum(p, axis=-1, keepdims=True) for p in parts])


MOD_TN = 1024


def _mod_kernel(c_ref, w_ref, b_ref, o_ref):
    c = c_ref[...]
    cond = c * (1.0 / (1.0 + jnp.exp(-c)))
    o_ref[0] = _dot(cond.astype(BF16), w_ref[0].astype(BF16)) + b_ref[0]


def _mod_call(c_rows, w_mod, b_mod):
    n = N_MOD * D_MODEL
    return pl.pallas_call(
        _mod_kernel,
        grid=(DEPTH, n // MOD_TN),
        in_specs=[
            pl.BlockSpec((MOD_ROWS, D_MODEL), lambda l, j: (0, 0)),
            pl.BlockSpec((1, D_MODEL, MOD_TN), lambda l, j: (l, 0, j)),
            pl.BlockSpec((1, 1, MOD_TN), lambda l, j: (l, 0, j)),
        ],
        out_specs=pl.BlockSpec((1, MOD_ROWS, MOD_TN), lambda l, j: (l, 0, j)),
        out_shape=jax.ShapeDtypeStruct((DEPTH, MOD_ROWS, n), F32),
        compiler_params=_params(("arbitrary", "arbitrary"), 40),
        name="mod",
    )(c_rows, w_mod, b_mod.reshape(DEPTH, 1, n))


PROJ_TM = 512


def _proj_kernel(x_ref, shift_ref, scale_ref, g_ref, w_ref, rope_ref, hg_ref, o_ref, *, tiles):
    h = _rms(x_ref[...], g_ref[...]) * (1.0 + scale_ref[0]) + shift_ref[0]
    hb = h.astype(BF16)
    per = COL_TILE // LANES
    for n, t in enumerate(tiles):
        y = _dot(hb, w_ref[:, t * COL_TILE:(t + 1) * COL_TILE])
        for j in range(per):
            gain_row, rope, q_scale = BLOCK_KIND[t * per + j]
            yj = y[:, j * LANES:(j + 1) * LANES]
            if gain_row is not None:
                yj = _rms(yj, hg_ref[gain_row:gain_row + 1, :])
            if rope is not None:
                base = rope * 3 * LANES
                cos = rope_ref[:, base:base + LANES]
                sin_next = rope_ref[:, base + LANES:base + 2 * LANES]
                sin_prev = rope_ref[:, base + 2 * LANES:base + 3 * LANES]
                yj = (yj * cos + pltpu.roll(yj, LANES - 1, 1) * sin_next
                      + pltpu.roll(yj, 1, 1) * sin_prev)
            if q_scale is not None:
                yj = yj * q_scale
            o_ref[:, (n * per + j) * LANES:(n * per + j + 1) * LANES] = yj.astype(BF16)


def _proj_call(x2d, mod, g, w_in, layer, rope, head_gains, *, tiles, mod_row, rope_tile):
    t_tok = x2d.shape[0]
    tm = PROJ_TM
    wout = len(tiles) * COL_TILE
    return pl.pallas_call(
        functools.partial(_proj_kernel, tiles=tiles),
        grid=(t_tok // tm,),
        in_specs=[
            pl.BlockSpec((tm, D_MODEL), lambda i: (i, 0)),
            pl.BlockSpec((None, 1, 1, D_MODEL), lambda i: (layer, mod_row(i) * N_MOD + 0, 0, 0)),
            pl.BlockSpec((None, 1, 1, D_MODEL), lambda i: (layer, mod_row(i) * N_MOD + 1, 0, 0)),
            pl.BlockSpec((None, 1, D_MODEL), lambda i: (layer, 0, 0)),
            pl.BlockSpec((None, D_MODEL, PROJ_WIDTH), lambda i: (layer, 0, 0), pipeline_mode=pl.Buffered(1)),
            pl.BlockSpec((tm, 6 * LANES), lambda i: (rope_tile(i), 0)),
            pl.BlockSpec((None, 8, LANES), lambda i: (layer, 0, 0)),
        ],
        out_specs=pl.BlockSpec((tm, wout), lambda i: (i, 0)),
        out_shape=jax.ShapeDtypeStruct((t_tok, wout), BF16),
        compiler_params=_params(("arbitrary",), 56),
        name="proj",
    )(x2d, mod, mod, g, w_in, rope, head_gains)


A_TQ = 512


def _attn_a_kernel(q_ref, k_ref, v_ref, kc_ref, vc_ref, o_ref):
    k, v, kc, vc = k_ref[0], v_ref[0], kc_ref[0], vc_ref[0]
    for g in range(2):
        q = q_ref[0, :, g * LANES:(g + 1) * LANES]
        sx = _dot_t(k, q)
        sc = _dot_t(kc, q)
        m = _colmax([sx, sc])
        ex = jnp.exp2(sx - m)
        ec = jnp.exp2(sc - m)
        den = _colsum([ex, ec])
        o_t = _dot_tn(v, ex.astype(BF16)) + _dot_tn(vc, ec.astype(BF16))
        o_ref[0, :, g * LANES:(g + 1) * LANES] = (o_t / den).T.astype(BF16)


def _attn_a_call(px, pc, ctx_tiles):
    b = px.shape[0]
    ak, av = SEG["ak"], SEG["av"]
    akc, avc = _seg_block(ctx_tiles, "ak"), _seg_block(ctx_tiles, "av")
    return pl.pallas_call(
        _attn_a_kernel,
        grid=(b, 2, SEQ // A_TQ),
        in_specs=[
            pl.BlockSpec((1, A_TQ, 2 * LANES), lambda i, h, t: (i, t, h)),
            pl.BlockSpec((1, SEQ, LANES), lambda i, h, t: (i, 0, ak + h)),
            pl.BlockSpec((1, SEQ, LANES), lambda i, h, t: (i, 0, av + h)),
            pl.BlockSpec((1, CTX_LEN, LANES), lambda i, h, t: (i, 0, akc + h)),
            pl.BlockSpec((1, CTX_LEN, LANES), lambda i, h, t: (i, 0, avc + h)),
        ],
        out_specs=pl.BlockSpec((1, A_TQ, 2 * LANES), lambda i, h, t: (i, t, h)),
        out_shape=jax.ShapeDtypeStruct((b, SEQ, GROUP_WIDTH), BF16),
        compiler_params=_params(("arbitrary",) * 3, 48),
        name="attn_a",
    )(px, px, px, pc, pc)


NB_ROWS = SEQ // GRID_W
NB_QROWS = 4
NB_KROWS = NB_QROWS + WIN_ROWS - 1
NB_BLOCKS = NB_ROWS // NB_QROWS
NB_TQ = NB_QROWS * GRID_W
NB_KEYS = NB_KROWS * GRID_W


def _nb_window_start(j):
    return min(max(j * NB_QROWS - WIN_ROWS // 2, 0), NB_ROWS - NB_KROWS)


def _nb_class(j):
    return 0 if j == 0 else (2 if j == NB_BLOCKS - 1 else 1)


def _na_bias_table(rpb):
    n_dr, n_dc = rpb.shape[-2:]
    rpb = rpb.astype(F32).reshape(-1, n_dr, n_dc)
    n_tab = rpb.shape[0]
    period = 2 * GRID_W
    fill = jnp.full((n_tab, n_dr, period - n_dc), NEG_INF, F32)
    ext = jnp.concatenate([rpb[..., WIN_COLS - 1:], fill, rpb[..., :WIN_COLS - 1]], axis=-1)
    flat = jnp.tile(ext, (1, 1, GRID_W))[..., :GRID_W * (period - 1)]
    toep = flat.reshape(n_tab, n_dr, GRID_W, period - 1)[..., :GRID_W]
    c = jnp.arange(GRID_W)
    c_start = jnp.clip(c - WIN_COLS // 2, 0, GRID_W - WIN_COLS)
    kc = jnp.arange(GRID_W)
    in_win = (kc[None, :] >= c_start[:, None]) & (kc[None, :] < c_start[:, None] + WIN_COLS)
    toep = jnp.where(in_win, toep * LOG2E, NEG_INF)

    def masked_rows(n):
        return jnp.full((n_tab, n, GRID_W, GRID_W), NEG_INF, F32)

    classes = []
    for j in (0, 1, NB_BLOCKS - 1):
        q_rows = []
        for rq in range(NB_QROWS):
            r = j * NB_QROWS + rq
            r_start = min(max(r - WIN_ROWS // 2, 0), NB_ROWS - WIN_ROWS)
            lo = r_start - _nb_window_start(j)
            drow0 = r_start - r + WIN_ROWS - 1
            pieces = [masked_rows(lo), toep[:, drow0:drow0 + WIN_ROWS], masked_rows(NB_KROWS - WIN_ROWS - lo)]
            q_rows.append(jnp.concatenate([p for p in pieces if p.shape[1]], axis=1))
        classes.append(jnp.stack(q_rows, axis=1))
    table = jnp.stack(classes, axis=1).transpose(0, 1, 2, 4, 3, 5)
    return table.reshape(n_tab, 3, NB_TQ, NB_KEYS)


def _attn_b_kernel(q_ref, k_ref, v_ref, kc_ref, vc_ref, bias_ref, o_ref):
    kc, vc = kc_ref[0], vc_ref[0]
    for j in range(NB_BLOCKS):
        q0 = j * NB_TQ
        k0 = _nb_window_start(j) * GRID_W
        q = q_ref[0, q0:q0 + NB_TQ, :]
        kw = k_ref[0, k0:k0 + NB_KEYS, :]
        vw = v_ref[0, k0:k0 + NB_KEYS, :]
        sn = _dot_t(q, kw) + bias_ref[0, _nb_class(j)]
        sc = _dot_t(q, kc)
        m = _rowmax([sn, sc])
        en = jnp.exp2(sn - m)
        ec = jnp.exp2(sc - m)
        den = _rowsum([en, ec])
        o = _dot(en.astype(BF16), vw) + _dot(ec.astype(BF16), vc)
        o_ref[0, q0:q0 + NB_TQ, :] = (o / den).astype(BF16)


def _attn_b_call(px, pc, ctx_tiles, bias, layer):
    b = px.shape[0]
    bq, bk, bv = SEG["bq"], SEG["bk"], SEG["bv"]
    bkc, bvc = _seg_block(ctx_tiles, "bk"), _seg_block(ctx_tiles, "bv")
    return pl.pallas_call(
        _attn_b_kernel,
        grid=(b, 4),
        in_specs=[
            pl.BlockSpec((1, SEQ, LANES), lambda i, h: (i, 0, bq + h)),
            pl.BlockSpec((1, SEQ, LANES), lambda i, h: (i, 0, bk + h)),
            pl.BlockSpec((1, SEQ, LANES), lambda i, h: (i, 0, bv + h)),
            pl.BlockSpec((1, CTX_LEN, LANES), lambda i, h: (i, 0, bkc + h)),
            pl.BlockSpec((1, CTX_LEN, LANES), lambda i, h: (i, 0, bvc + h)),
            pl.BlockSpec((1, 3, NB_TQ, NB_KEYS), lambda i, h: (layer * 4 + h, 0, 0, 0)),
        ],
        out_specs=pl.BlockSpec((1, SEQ, LANES), lambda i, h: (i, 0, h)),
        out_shape=jax.ShapeDtypeStruct((b, SEQ, GROUP_WIDTH), BF16),
        compiler_params=_params(("arbitrary",) * 2, 32),
        name="attn_b",
    )(px, px, px, pc, pc, bias)


C_TQ = 512


def _diff_lambda(lam_ref, lam_init):
    a = jnp.sum(lam_ref[0:1, :] * lam_ref[1:2, :], axis=-1, keepdims=True)
    b = jnp.sum(lam_ref[2:3, :] * lam_ref[3:4, :], axis=-1, keepdims=True)
    return jnp.exp(a) - jnp.exp(b) + lam_init


def _diff_head(q, keys, vals, lam, gsub, lam_init):
    lane = lax.broadcasted_iota(jnp.int32, q.shape, 1)
    zero = jnp.zeros_like(q)
    maps = []
    for sub in range(2):
        qs = jnp.where((lane < DIFF_QK_DIM) == (sub == 0), q, zero)
        s = [_dot_t(k, qs) for k in keys]
        m = _colmax(s)
        e = [jnp.exp2(x - m) for x in s]
        maps.append((e, _colsum(e)))
    (e1, den1), (e2, den2) = maps
    ratio = lam * den1 / den2
    o_t = functools.reduce(jnp.add, [
        _dot_tn(v, (a - ratio * b).astype(BF16)) for a, b, v in zip(e1, e2, vals)]) / den1
    return _rms(o_t.T, gsub) * (1.0 - lam_init)


def _attn_c_kernel(lam_ref, gsub_ref, q_ref, k_ref, v_ref, kc_ref, vc_ref, o_ref, *, lam_init):
    lam = _diff_lambda(lam_ref, lam_init)
    o = _diff_head(q_ref[0], [k_ref[0], kc_ref[0]], [v_ref[0], vc_ref[0]], lam, gsub_ref[...], lam_init)
    o_ref[0] = o.astype(BF16)


def _attn_c_call(px, pc, ctx_tiles, lam_rows, gsub, layer, lam_init):
    b = px.shape[0]
    cq, ck, cv = SEG["cq"], SEG["ck"], SEG["cv"]
    ckc, cvc = _seg_block(ctx_tiles, "ck"), _seg_block(ctx_tiles, "cv")
    return pl.pallas_call(
        functools.partial(_attn_c_kernel, lam_init=lam_init),
        grid=(b, 4, SEQ // C_TQ),
        in_specs=[
            pl.BlockSpec((None, 8, LANES), lambda i, h, t: (layer, 0, 0)),
            pl.BlockSpec((None, 1, LANES), lambda i, h, t: (layer, 0, 0)),
            pl.BlockSpec((1, C_TQ, LANES), lambda i, h, t: (i, t, cq + h)),
            pl.BlockSpec((1, SEQ, LANES), lambda i, h, t: (i, 0, ck + h)),
            pl.BlockSpec((1, SEQ, LANES), lambda i, h, t: (i, 0, cv + h)),
            pl.BlockSpec((1, CTX_LEN, LANES), lambda i, h, t: (i, 0, ckc + h)),
            pl.BlockSpec((1, CTX_LEN, LANES), lambda i, h, t: (i, 0, cvc + h)),
        ],
        out_specs=pl.BlockSpec((1, C_TQ, LANES), lambda i, h, t: (i, t, h)),
        out_shape=jax.ShapeDtypeStruct((b, SEQ, GROUP_WIDTH), BF16),
        compiler_params=_params(("arbitrary",) * 3, 48),
        name="attn_c",
    )(lam_rows, gsub, px, px, px, pc, pc)


D_TQ = 256
D_SPAN = D_TQ + 2 * WINDOW


def _attn_d_kernel(sink_ref, q_ref, k_ref, v_ref, kc_ref, vc_ref, o_ref, *, layer):
    kvh = pl.program_id(1)
    kc, vc = kc_ref[0], vc_ref[0]
    row = lax.broadcasted_iota(jnp.int32, (2 * D_TQ, D_SPAN), 0)
    rel = (row & (D_TQ - 1)) - lax.broadcasted_iota(jnp.int32, (2 * D_TQ, D_SPAN), 1)
    head_row = lax.broadcasted_iota(jnp.int32, (2 * D_TQ, 1), 0)
    sink = jnp.where(head_row < D_TQ, sink_ref[layer, kvh * 2], sink_ref[layer, kvh * 2 + 1]) * LOG2E
    for i in range(SEQ // D_TQ):
        q0 = i * D_TQ
        k0 = min(max(q0 - WINDOW, 0), SEQ - D_SPAN)
        kw = k_ref[0, k0:k0 + D_SPAN, :]
        vw = v_ref[0, k0:k0 + D_SPAN, :]
        valid = jnp.abs(rel + (q0 - k0)) <= WINDOW
        q = jnp.concatenate([q_ref[0, q0:q0 + D_TQ, :LANES], q_ref[0, q0:q0 + D_TQ, LANES:]], axis=0)
        sw = jnp.where(valid, _dot_t(q, kw), NEG_INF)
        sc = _dot_t(q, kc)
        m = jnp.maximum(_rowmax([sw, sc]), sink)
        ew = jnp.exp2(sw - m)
        ec = jnp.exp2(sc - m)
        den = _rowsum([ew, ec]) + jnp.exp2(sink - m)
        o = ((_dot(ew.astype(BF16), vw) + _dot(ec.astype(BF16), vc)) / den).astype(BF16)
        o_ref[0, q0:q0 + D_TQ, :LANES] = o[:D_TQ]
        o_ref[0, q0:q0 + D_TQ, LANES:] = o[D_TQ:]


def _attn_d_call(px, pc, ctx_tiles, sink, layer):
    b = px.shape[0]
    dq, dk, dv = SEG["dq"], SEG["dk"], SEG["dv"]
    dkc, dvc = _seg_block(ctx_tiles, "dk"), _seg_block(ctx_tiles, "dv")
    return pl.pallas_call(
        functools.partial(_attn_d_kernel, layer=layer),
        grid=(b, 2),
        in_specs=[
            pl.BlockSpec(memory_space=pltpu.SMEM),
            pl.BlockSpec((1, SEQ, 2 * LANES), lambda i, h: (i, 0, dq // 2 + h)),
            pl.BlockSpec((1, SEQ, LANES), lambda i, h: (i, 0, dk + h)),
            pl.BlockSpec((1, SEQ, LANES), lambda i, h: (i, 0, dv + h)),
            pl.BlockSpec((1, CTX_LEN, LANES), lambda i, h: (i, 0, dkc + h)),
            pl.BlockSpec((1, CTX_LEN, LANES), lambda i, h: (i, 0, dvc + h)),
        ],
        out_specs=pl.BlockSpec((1, SEQ, 2 * LANES), lambda i, h: (i, 0, h)),
        out_shape=jax.ShapeDtypeStruct((b, SEQ, GROUP_WIDTH), BF16),
        compiler_params=_params(("arbitrary",) * 2, 32),
        name="attn_d",
    )(sink, px, px, px, pc, pc)


def _softmax_pv(q, k, v, sink=None):
    s = _dot_t(q, k)
    m = jnp.max(s, axis=-1, keepdims=True)
    if sink is not None:
        m = jnp.maximum(m, sink)
    e = jnp.exp2(s - m)
    den = jnp.sum(e, axis=-1, keepdims=True)
    if sink is not None:
        den = den + jnp.exp2(sink - m)
    return _dot(e.astype(BF16), v) / den


def _ctx_attn_kernel(sink_ref, lam_ref, gsub_ref, p_ref, o_ref, *, layer, lam_init):
    def blk(name, h):
        c = (SEG[name] + h) * LANES
        return p_ref[0, :, c:c + LANES]

    def put(group, h, o):
        c = group * GROUP_WIDTH + h * LANES
        o_ref[0, :, c:c + LANES] = o.astype(BF16)

    lam = _diff_lambda(lam_ref, lam_init)
    for h in range(4):
        put(0, h, _softmax_pv(blk("aq", h), blk("ak", h // 2), blk("av", h // 2)))
        put(1, h, _softmax_pv(blk("bq", h), blk("bk", h), blk("bv", h)))
        put(2, h, _diff_head(blk("cq", h), [blk("ck", h)], [blk("cv", h)], lam, gsub_ref[...], lam_init))
        put(3, h, _softmax_pv(blk("dq", h), blk("dk", h // 2), blk("dv", h // 2),
                              sink=sink_ref[layer, h] * LOG2E))


def _ctx_attn_call(pc, sink, lam_rows, gsub, layer, lam_init):
    b = pc.shape[0]
    return pl.pallas_call(
        functools.partial(_ctx_attn_kernel, layer=layer, lam_init=lam_init),
        grid=(b,),
        in_specs=[
            pl.BlockSpec(memory_space=pltpu.SMEM),
            pl.BlockSpec((None, 8, LANES), lambda i: (layer, 0, 0)),
            pl.BlockSpec((None, 1, LANES), lambda i: (layer, 0, 0)),
            pl.BlockSpec((1, CTX_LEN, PROJ_WIDTH), lambda i: (i, 0, 0)),
        ],
        out_specs=pl.BlockSpec((1, CTX_LEN, D_MODEL), lambda i: (i, 0, 0)),
        out_shape=jax.ShapeDtypeStruct((b, CTX_LEN, D_MODEL), BF16),
        compiler_params=_params(("arbitrary",), 32),
        name="ctx_attn",
    )(sink, lam_rows, gsub, pc)


POST_TM = 512
POST_TH = 1024


def _post_kernel(x_ref, oa_ref, ob_ref, oc_ref, od_ref, wout_ref, gate_mix_ref, shift_ref, scale_ref,
                 gate_mlp_ref, g_ref, wup_ref, wdn_ref, gfin_ref, out_ref, x1_ref, h_ref, *, final):
    j = pl.program_id(1)

    @pl.when(j == 0)
    def _():
        mix = functools.reduce(jnp.add, [
            _dot(o[...], wout_ref[n * GROUP_WIDTH:(n + 1) * GROUP_WIDTH, :])
            for n, o in enumerate((oa_ref, ob_ref, oc_ref, od_ref))])
        x1 = x_ref[...] + gate_mix_ref[0] * mix
        x1_ref[...] = x1
        h_ref[...] = (_rms(x1, g_ref[...]) * (1.0 + scale_ref[0]) + shift_ref[0]).astype(BF16)
        out_ref[...] = jnp.zeros_like(out_ref)

    u = _dot(h_ref[...], wup_ref[...])
    act = jnp.square(jnp.maximum(u, 0.0)).astype(BF16)
    out_ref[...] += _dot(act, wdn_ref[...])

    @pl.when(j == pl.num_programs(1) - 1)
    def _():
        y = x1_ref[...] + gate_mlp_ref[0] * out_ref[...]
        if final:
            y = _rms(y, gfin_ref[...])
        out_ref[...] = y


def _post_call(x2d, mix_parts, mix_blocks, w_out, mod, g_mlp, w_up_tiled, w_down, layer, g_final, *,
               mod_row, final):
    t_tok = x2d.shape[0]
    tm, th = POST_TM, POST_TH
    once = pl.Buffered(1)

    def mod_spec(k):
        return pl.BlockSpec((None, 1, 1, D_MODEL), lambda i, j: (layer, mod_row(i) * N_MOD + k, 0, 0))

    def part_spec(blk):
        return pl.BlockSpec((tm, GROUP_WIDTH), lambda i, j: (i, blk))

    return pl.pallas_call(
        functools.partial(_post_kernel, final=final),
        grid=(t_tok // tm, MLP_HIDDEN // th),
        in_specs=[
            pl.BlockSpec((tm, D_MODEL), lambda i, j: (i, 0)),
            *[part_spec(blk) for blk in mix_blocks],
            pl.BlockSpec((None, D_MODEL, D_MODEL), lambda i, j: (layer, 0, 0), pipeline_mode=once),
            mod_spec(2), mod_spec(3), mod_spec(4), mod_spec(5),
            pl.BlockSpec((None, 1, D_MODEL), lambda i, j: (layer, 0, 0)),
            pl.BlockSpec((None, None, D_MODEL, th), lambda i, j: (layer, j, 0, 0)),
            pl.BlockSpec((None, th, D_MODEL), lambda i, j: (layer, j, 0)),
            pl.BlockSpec((1, D_MODEL), lambda i, j: (0, 0)),
        ],
        out_specs=pl.BlockSpec((tm, D_MODEL), lambda i, j: (i, 0)),
        out_shape=jax.ShapeDtypeStruct((t_tok, D_MODEL), F32),
        scratch_shapes=[pltpu.VMEM((tm, D_MODEL), F32), pltpu.VMEM((tm, D_MODEL), BF16)],
        compiler_params=_params(("arbitrary", "arbitrary"), 60),
        name="post",
    )(x2d, *mix_parts, w_out, mod, mod, mod, mod, g_mlp, w_up_tiled, w_down, g_final)


def _rope_rows(n_tok, dim):
    t = jnp.arange(n_tok)
    row = (t // GRID_W).astype(F32)
    col = (t % GRID_W).astype(F32)
    n_freq = dim // 4
    inv_freq = ROPE_THETA ** (-jnp.arange(n_freq, dtype=F32) / n_freq)
    ang = jnp.concatenate([row[:, None] * inv_freq, col[:, None] * inv_freq], axis=-1)
    cos = jnp.repeat(jnp.cos(ang), 2, axis=-1)
    sin = jnp.repeat(jnp.sin(ang), 2, axis=-1)
    even = (jnp.arange(dim) % 2) == 0
    parts = [cos, jnp.where(even, -sin, 0.0), jnp.where(even, 0.0, sin)]
    return [jnp.tile(p, (1, LANES // dim)) for p in parts]


def _rope_table(n_tok):
    return jnp.concatenate(_rope_rows(n_tok, HEAD_DIM) + _rope_rows(n_tok, DIFF_QK_DIM), axis=-1)


def _identity_rope_table(n_tok):
    one = jnp.ones((n_tok, LANES), F32)
    zero = jnp.zeros((n_tok, LANES), F32)
    return jnp.concatenate([one, zero, zero] * 2, axis=-1)


def _pad_rows(rows, n_rows=8):
    depth = rows[0].shape[0]
    padded = [jnp.pad(r.astype(F32), ((0, 0), (0, LANES - r.shape[1]))) for r in rows]
    padded += [jnp.zeros((depth, LANES), F32)] * (n_rows - len(rows))
    return jnp.stack(padded, axis=1)


def kernel(x, c, ctx, c_ctx, g_mix, g_mlp, w_mod, b_mod, w_in, w_out, gqa_gq, gqa_gk, na_rpb,
           diff_lq1, diff_lk1, diff_lq2, diff_lk2, diff_gsub, swa_sink, w_up, w_down, g_final):
    b, s, d = x.shape
    assert (s, d) == (SEQ, D_MODEL) and ctx.shape == (b, CTX_LEN, D_MODEL) and b <= CTX_MOD_ROW

    c_rows = jnp.concatenate(
        [c, jnp.zeros((CTX_MOD_ROW - b, D_MODEL), F32), c_ctx[None],
         jnp.zeros((MOD_ROWS - CTX_MOD_ROW - 1, D_MODEL), F32)], axis=0)
    mod = _mod_call(c_rows, w_mod, b_mod).reshape(DEPTH, MOD_ROWS * N_MOD, 1, D_MODEL)

    w_in_b = w_in.astype(BF16)
    w_out_b = w_out.astype(BF16)
    w_up_b = (w_up.astype(BF16).reshape(DEPTH, D_MODEL, MLP_HIDDEN // POST_TH, POST_TH)
              .transpose(0, 2, 1, 3))
    w_down_b = w_down.astype(BF16)

    rope_x = _rope_table(SEQ)
    rope_c = _identity_rope_table(PROJ_TM)
    x_tiles_per_seq = SEQ // PROJ_TM
    x_post_tiles_per_seq = SEQ // POST_TM
    g_final2 = g_final.reshape(1, D_MODEL)
    g_mix3 = g_mix.reshape(DEPTH, 1, D_MODEL)
    g_mlp3 = g_mlp.reshape(DEPTH, 1, D_MODEL)
    head_gains = _pad_rows([gqa_gq, gqa_gk])
    lam_rows = _pad_rows([diff_lq1, diff_lk1, diff_lq2, diff_lk2])
    gsub = diff_gsub.reshape(DEPTH, 1, HEAD_DIM)
    sink = swa_sink.astype(F32)
    nb_bias = _na_bias_table(na_rpb)

    x2d = x.reshape(b * s, D_MODEL)
    c2d = ctx.reshape(b * CTX_LEN, D_MODEL)
    for l in range(DEPTH):
        with_ctx = l < DEPTH - 1
        lam_init = 0.8 - 0.6 * math.exp(-0.3 * l)
        ctx_tiles = ALL_TILES if with_ctx else KV_TILES

        px = _proj_call(x2d, mod, g_mix3, w_in_b, l, rope_x, head_gains, tiles=ALL_TILES,
                        mod_row=lambda i: i // x_tiles_per_seq, rope_tile=lambda i: i % x_tiles_per_seq)
        pc = _proj_call(c2d, mod, g_mix3, w_in_b, l, rope_c, head_gains, tiles=ctx_tiles,
                        mod_row=lambda i: CTX_MOD_ROW, rope_tile=lambda i: 0)
        px = px.reshape(b, SEQ, PROJ_WIDTH)
        pc = pc.reshape(b, CTX_LEN, len(ctx_tiles) * COL_TILE)

        oa = _attn_a_call(px, pc, ctx_tiles)
        ob = _attn_b_call(px, pc, ctx_tiles, nb_bias, l)
        oc = _attn_c_call(px, pc, ctx_tiles, lam_rows, gsub, l, lam_init)
        od = _attn_d_call(px, pc, ctx_tiles, sink, l)
        parts = [o.reshape(b * s, GROUP_WIDTH) for o in (oa, ob, oc, od)]
        if with_ctx:
            o_ctx = _ctx_attn_call(pc, sink, lam_rows, gsub, l, lam_init).reshape(b * CTX_LEN, D_MODEL)

        x2d = _post_call(x2d, parts, (0, 0, 0, 0), w_out_b, mod, g_mlp3, w_up_b, w_down_b, l, g_final2,
                         mod_row=lambda i: i // x_post_tiles_per_seq, final=not with_ctx)
        if with_ctx:
            c2d = _post_call(c2d, [o_ctx] * 4, (0, 1, 2, 3), w_out_b, mod, g_mlp3, w_up_b, w_down_b, l,
                             g_final2, mod_row=lambda i: CTX_MOD_ROW, final=False)
    return x2d.reshape(b, s, D_MODEL)
```

```python
import functools
import math

import jax
import jax.numpy as jnp
from jax import lax
from jax.experimental import pallas as pl
from jax.experimental.pallas import tpu as pltpu

D_MODEL = 2048
SEQ = 2048
DEPTH = 2
GRID_W = 64
CTX_LEN = 256
HEAD_DIM = 128
GROUP_WIDTH = 512
WIN_ROWS = 8
WIN_COLS = 16
DIFF_QK_DIM = 64
WINDOW = 128
MLP_HIDDEN = 4 * D_MODEL
N_MOD = 6
ROPE_THETA = 10000.0
NORM_EPS = 1e-6
NEG_INF = -1e30
PROJ_WIDTH = 5120
LOG2E = math.log2(math.e)

LANES = 128
COL_TILE = 512
MOD_ROWS = 16
CTX_MOD_ROW = 8

F32 = jnp.float32
BF16 = jnp.bfloat16

SEG = dict(aq=0, ak=4, av=6, bq=8, bk=12, bv=16, cq=20, ck=24, cv=28, dq=32, dk=36, dv=38)
ALL_TILES = tuple(range(PROJ_WIDTH // COL_TILE))
KV_TILES = (1, 3, 4, 6, 7, 9)
Q_SCALE_H = HEAD_DIM ** -0.5 * LOG2E
Q_SCALE_D = DIFF_QK_DIM ** -0.5 * LOG2E
_ROPE_H, _ROPE_D = 0, 1
BLOCK_KIND = {}
for _b in range(PROJ_WIDTH // LANES):
    if _b < 4:
        BLOCK_KIND[_b] = (0, _ROPE_H, Q_SCALE_H)
    elif _b < 6:
        BLOCK_KIND[_b] = (1, _ROPE_H, None)
    elif 8 <= _b < 12:
        BLOCK_KIND[_b] = (None, None, Q_SCALE_H)
    elif 20 <= _b < 24:
        BLOCK_KIND[_b] = (None, _ROPE_D, Q_SCALE_D)
    elif 24 <= _b < 28:
        BLOCK_KIND[_b] = (None, _ROPE_D, None)
    elif 32 <= _b < 36:
        BLOCK_KIND[_b] = (None, _ROPE_H, Q_SCALE_H)
    elif 36 <= _b < 38:
        BLOCK_KIND[_b] = (None, _ROPE_H, None)
    else:
        BLOCK_KIND[_b] = (None, None, None)


def _seg_block(tiles, name):
    b = SEG[name]
    t, r = divmod(b, COL_TILE // LANES)
    return tiles.index(t) * (COL_TILE // LANES) + r


def _params(dims, vmem_mb, flags=None):
    return pltpu.CompilerParams(dimension_semantics=dims, vmem_limit_bytes=vmem_mb * 1024 * 1024, flags=flags)


def _rms(x, g):
    ms = jnp.mean(x * x, axis=-1, keepdims=True)
    return x * lax.rsqrt(ms + NORM_EPS) * g


def _dot(a, b):
    return jnp.dot(a, b, preferred_element_type=F32)


def _dot_t(a, b):
    return lax.dot_general(a, b, (((1,), (1,)), ((), ())), preferred_element_type=F32)


def _dot_tn(a, b):
    return lax.dot_general(a, b, (((0,), (0,)), ((), ())), preferred_element_type=F32)


def _colmax(parts):
    return functools.reduce(jnp.maximum, [jnp.max(p, axis=0, keepdims=True) for p in parts])


def _colsum(parts):
    return functools.reduce(jnp.add, [jnp.sum(p, axis=0, keepdims=True) for p in parts])


def _rowmax(parts):
    return functools.reduce(jnp.maximum, [jnp.max(p, axis=-1, keepdims=True) for p in parts])


def _rowsum(parts):
    return functools.reduce(jnp.add, [jnp.sum(p, axis=-1, keepdims=True) for p in parts])


MOD_TN = 1024


def _mod_kernel(c_ref, w_ref, b_ref, o_ref):
    c = c_ref[...]
    cond = c * (1.0 / (1.0 + jnp.exp(-c)))
    o_ref[0] = _dot(cond.astype(BF16), w_ref[0].astype(BF16)) + b_ref[0]


def _mod_call(c_rows, w_mod, b_mod):
    n = N_MOD * D_MODEL
    return pl.pallas_call(
        _mod_kernel,
        grid=(DEPTH, n // MOD_TN),
        in_specs=[
            pl.BlockSpec((MOD_ROWS, D_MODEL), lambda l, j: (0, 0)),
            pl.BlockSpec((1, D_MODEL, MOD_TN), lambda l, j: (l, 0, j)),
            pl.BlockSpec((1, 1, MOD_TN), lambda l, j: (l, 0, j)),
        ],
        out_specs=pl.BlockSpec((1, MOD_ROWS, MOD_TN), lambda l, j: (l, 0, j)),
        out_shape=jax.ShapeDtypeStruct((DEPTH, MOD_ROWS, n), F32),
        compiler_params=_params(("arbitrary", "arbitrary"), 40),
        name="mod",
    )(c_rows, w_mod, b_mod.reshape(DEPTH, 1, n))


PROJ_TM = 512


def _proj_kernel(x_ref, shift_ref, scale_ref, g_ref, w_ref, rope_ref, hg_ref, o_ref, *, tiles):
    h = _rms(x_ref[...], g_ref[...]) * (1.0 + scale_ref[0]) + shift_ref[0]
    hb = h.astype(BF16)
    per = COL_TILE // LANES
    for n, t in enumerate(tiles):
        y = _dot(hb, w_ref[:, t * COL_TILE:(t + 1) * COL_TILE])
        for j in range(per):
            gain_row, rope, q_scale = BLOCK_KIND[t * per + j]
            yj = y[:, j * LANES:(j + 1) * LANES]
            if gain_row is not None:
                yj = _rms(yj, hg_ref[gain_row:gain_row + 1, :])
            if rope is not None:
                base = rope * 3 * LANES
                cos = rope_ref[:, base:base + LANES]
                sin_next = rope_ref[:, base + LANES:base + 2 * LANES]
                sin_prev = rope_ref[:, base + 2 * LANES:base + 3 * LANES]
                yj = (yj * cos + pltpu.roll(yj, LANES - 1, 1) * sin_next
                      + pltpu.roll(yj, 1, 1) * sin_prev)
            if q_scale is not None:
                yj = yj * q_scale
            o_ref[:, (n * per + j) * LANES:(n * per + j + 1) * LANES] = yj.astype(BF16)


def _proj_call(x2d, mod, g, w_in, layer, rope, head_gains, *, tiles, mod_row, rope_tile):
    t_tok = x2d.shape[0]
    tm = PROJ_TM
    wout = len(tiles) * COL_TILE
    return pl.pallas_call(
        functools.partial(_proj_kernel, tiles=tiles),
        grid=(t_tok // tm,),
        in_specs=[
            pl.BlockSpec((tm, D_MODEL), lambda i: (i, 0)),
            pl.BlockSpec((None, 1, 1, D_MODEL), lambda i: (layer, mod_row(i) * N_MOD + 0, 0, 0)),
            pl.BlockSpec((None, 1, 1, D_MODEL), lambda i: (layer, mod_row(i) * N_MOD + 1, 0, 0)),
            pl.BlockSpec((None, 1, D_MODEL), lambda i: (layer, 0, 0)),
            pl.BlockSpec((None, D_MODEL, PROJ_WIDTH), lambda i: (layer, 0, 0), pipeline_mode=pl.Buffered(1)),
            pl.BlockSpec((tm, 6 * LANES), lambda i: (rope_tile(i), 0)),
            pl.BlockSpec((None, 8, LANES), lambda i: (layer, 0, 0)),
        ],
        out_specs=pl.BlockSpec((tm, wout), lambda i: (i, 0)),
        out_shape=jax.ShapeDtypeStruct((t_tok, wout), BF16),
        compiler_params=_params(("arbitrary",), 56),
        name="proj",
    )(x2d, mod, mod, g, w_in, rope, head_gains)


A_TQ = 512


def _attn_a_kernel(q_ref, k_ref, v_ref, kc_ref, vc_ref, o_ref):
    k, v, kc, vc = k_ref[0], v_ref[0], kc_ref[0], vc_ref[0]
    for r0 in range(0, SEQ, A_TQ):
        for g in range(2):
            q = q_ref[0, r0:r0 + A_TQ, g * LANES:(g + 1) * LANES]
            sx = _dot_t(k, q)
            sc = _dot_t(kc, q)
            m = _colmax([sx, sc])
            ex = jnp.exp2(sx - m)
            ec = jnp.exp2(sc - m)
            den = _colsum([ex, ec])
            o_t = _dot_tn(v, ex.astype(BF16)) + _dot_tn(vc, ec.astype(BF16))
            o_ref[0, r0:r0 + A_TQ, g * LANES:(g + 1) * LANES] = (o_t / den).T.astype(BF16)


def _attn_a_call(px, pc, ctx_tiles):
    b = px.shape[0]
    ak, av = SEG["ak"], SEG["av"]
    akc, avc = _seg_block(ctx_tiles, "ak"), _seg_block(ctx_tiles, "av")
    return pl.pallas_call(
        _attn_a_kernel,
        grid=(b, 2),
        in_specs=[
            pl.BlockSpec((1, SEQ, 2 * LANES), lambda i, h: (i, 0, h)),
            pl.BlockSpec((1, SEQ, LANES), lambda i, h: (i, 0, ak + h)),
            pl.BlockSpec((1, SEQ, LANES), lambda i, h: (i, 0, av + h)),
            pl.BlockSpec((1, CTX_LEN, LANES), lambda i, h: (i, 0, akc + h)),
            pl.BlockSpec((1, CTX_LEN, LANES), lambda i, h: (i, 0, avc + h)),
        ],
        out_specs=pl.BlockSpec((1, SEQ, 2 * LANES), lambda i, h: (i, 0, h)),
        out_shape=jax.ShapeDtypeStruct((b, SEQ, GROUP_WIDTH), BF16),
        compiler_params=_params(("arbitrary",) * 2, 48),
        name="attn_a",
    )(px, px, px, pc, pc)


NB_ROWS = SEQ // GRID_W
NB_QROWS = 4
NB_KROWS = NB_QROWS + WIN_ROWS - 1
NB_BLOCKS = NB_ROWS // NB_QROWS
NB_TQ = NB_QROWS * GRID_W
NB_KEYS = NB_KROWS * GRID_W


def _nb_window_start(j):
    return min(max(j * NB_QROWS - WIN_ROWS // 2, 0), NB_ROWS - NB_KROWS)


def _nb_class(j):
    return 0 if j == 0 else (2 if j == NB_BLOCKS - 1 else 1)


def _na_bias_table(rpb):
    n_dr, n_dc = rpb.shape[-2:]
    rpb = rpb.astype(F32).reshape(-1, n_dr, n_dc)
    n_tab = rpb.shape[0]
    period = 2 * GRID_W
    fill = jnp.full((n_tab, n_dr, period - n_dc), NEG_INF, F32)
    ext = jnp.concatenate([rpb[..., WIN_COLS - 1:], fill, rpb[..., :WIN_COLS - 1]], axis=-1)
    flat = jnp.tile(ext, (1, 1, GRID_W))[..., :GRID_W * (period - 1)]
    toep = flat.reshape(n_tab, n_dr, GRID_W, period - 1)[..., :GRID_W]
    c = jnp.arange(GRID_W)
    c_start = jnp.clip(c - WIN_COLS // 2, 0, GRID_W - WIN_COLS)
    kc = jnp.arange(GRID_W)
    in_win = (kc[None, :] >= c_start[:, None]) & (kc[None, :] < c_start[:, None] + WIN_COLS)
    toep = jnp.where(in_win, toep * LOG2E, NEG_INF)

    def masked_rows(n):
        return jnp.full((n_tab, n, GRID_W, GRID_W), NEG_INF, F32)

    classes = []
    for j in (0, 1, NB_BLOCKS - 1):
        q_rows = []
        for rq in range(NB_QROWS):
            r = j * NB_QROWS + rq
            r_start = min(max(r - WIN_ROWS // 2, 0), NB_ROWS - WIN_ROWS)
            lo = r_start - _nb_window_start(j)
            drow0 = r_start - r + WIN_ROWS - 1
            pieces = [masked_rows(lo), toep[:, drow0:drow0 + WIN_ROWS], masked_rows(NB_KROWS - WIN_ROWS - lo)]
            q_rows.append(jnp.concatenate([p for p in pieces if p.shape[1]], axis=1))
        classes.append(jnp.stack(q_rows, axis=1))
    table = jnp.stack(classes, axis=1).transpose(0, 1, 2, 4, 3, 5)
    return table.reshape(n_tab, 3, NB_TQ, NB_KEYS)


def _attn_b_kernel(q_ref, k_ref, v_ref, kc_ref, vc_ref, bias_ref, o_ref):
    kc, vc = kc_ref[0], vc_ref[0]
    for j in range(NB_BLOCKS):
        q0 = j * NB_TQ
        k0 = _nb_window_start(j) * GRID_W
        q = q_ref[0, q0:q0 + NB_TQ, :]
        kw = k_ref[0, k0:k0 + NB_KEYS, :]
        vw = v_ref[0, k0:k0 + NB_KEYS, :]
        sn = _dot_t(q, kw) + bias_ref[0, _nb_class(j)]
        sc = _dot_t(q, kc)
        m = _rowmax([sn, sc])
        en = jnp.exp2(sn - m)
        ec = jnp.exp2(sc - m)
        den = _rowsum([en, ec])
        o = _dot(en.astype(BF16), vw) + _dot(ec.astype(BF16), vc)
        o_ref[0, q0:q0 + NB_TQ, :] = (o / den).astype(BF16)


def _attn_b_call(px, pc, ctx_tiles, bias, layer):
    b = px.shape[0]
    bq, bk, bv = SEG["bq"], SEG["bk"], SEG["bv"]
    bkc, bvc = _seg_block(ctx_tiles, "bk"), _seg_block(ctx_tiles, "bv")
    return pl.pallas_call(
        _attn_b_kernel,
        grid=(b, 4),
        in_specs=[
            pl.BlockSpec((1, SEQ, LANES), lambda i, h: (i, 0, bq + h)),
            pl.BlockSpec((1, SEQ, LANES), lambda i, h: (i, 0, bk + h)),
            pl.BlockSpec((1, SEQ, LANES), lambda i, h: (i, 0, bv + h)),
            pl.BlockSpec((1, CTX_LEN, LANES), lambda i, h: (i, 0, bkc + h)),
            pl.BlockSpec((1, CTX_LEN, LANES), lambda i, h: (i, 0, bvc + h)),
            pl.BlockSpec((1, 3, NB_TQ, NB_KEYS), lambda i, h: (layer * 4 + h, 0, 0, 0)),
        ],
        out_specs=pl.BlockSpec((1, SEQ, LANES), lambda i, h: (i, 0, h)),
        out_shape=jax.ShapeDtypeStruct((b, SEQ, GROUP_WIDTH), BF16),
        compiler_params=_params(("arbitrary",) * 2, 32),
        name="attn_b",
    )(px, px, px, pc, pc, bias)


C_TQ = 512


def _diff_lambda(lam_ref, lam_init):
    a = jnp.sum(lam_ref[0:1, :] * lam_ref[1:2, :], axis=-1, keepdims=True)
    b = jnp.sum(lam_ref[2:3, :] * lam_ref[3:4, :], axis=-1, keepdims=True)
    return jnp.exp(a) - jnp.exp(b) + lam_init


def _diff_head(q, keys, vals, lam, gsub, lam_init):
    lane = lax.broadcasted_iota(jnp.int32, q.shape, 1)
    zero = jnp.zeros_like(q)
    maps = []
    for sub in range(2):
        qs = jnp.where((lane < DIFF_QK_DIM) == (sub == 0), q, zero)
        s = [_dot_t(k, qs) for k in keys]
        m = _colmax(s)
        e = [jnp.exp2(x - m) for x in s]
        maps.append((e, _colsum(e)))
    (e1, den1), (e2, den2) = maps
    ratio = lam * den1 / den2
    o_t = functools.reduce(jnp.add, [
        _dot_tn(v, (a - ratio * b).astype(BF16)) for a, b, v in zip(e1, e2, vals)]) / den1
    return _rms(o_t.T, gsub) * (1.0 - lam_init)


def _attn_c_kernel(lam_ref, gsub_ref, q_ref, k_ref, v_ref, kc_ref, vc_ref, o_ref, *, lam_init):
    lam = _diff_lambda(lam_ref, lam_init)
    keys, vals = [k_ref[0], kc_ref[0]], [v_ref[0], vc_ref[0]]
    for r0 in range(0, SEQ, C_TQ):
        o = _diff_head(q_ref[0, r0:r0 + C_TQ, :], keys, vals, lam, gsub_ref[...], lam_init)
        o_ref[0, r0:r0 + C_TQ, :] = o.astype(BF16)


def _attn_c_call(px, pc, ctx_tiles, lam_rows, gsub, layer, lam_init):
    b = px.shape[0]
    cq, ck, cv = SEG["cq"], SEG["ck"], SEG["cv"]
    ckc, cvc = _seg_block(ctx_tiles, "ck"), _seg_block(ctx_tiles, "cv")
    return pl.pallas_call(
        functools.partial(_attn_c_kernel, lam_init=lam_init),
        grid=(b, 4),
        in_specs=[
            pl.BlockSpec((None, 8, LANES), lambda i, h: (layer, 0, 0)),
            pl.BlockSpec((None, 1, LANES), lambda i, h: (layer, 0, 0)),
            pl.BlockSpec((1, SEQ, LANES), lambda i, h: (i, 0, cq + h)),
            pl.BlockSpec((1, SEQ, LANES), lambda i, h: (i, 0, ck + h)),
            pl.BlockSpec((1, SEQ, LANES), lambda i, h: (i, 0, cv + h)),
            pl.BlockSpec((1, CTX_LEN, LANES), lambda i, h: (i, 0, ckc + h)),
            pl.BlockSpec((1, CTX_LEN, LANES), lambda i, h: (i, 0, cvc + h)),
        ],
        out_specs=pl.BlockSpec((1, SEQ, LANES), lambda i, h: (i, 0, h)),
        out_shape=jax.ShapeDtypeStruct((b, SEQ, GROUP_WIDTH), BF16),
        compiler_params=_params(("arbitrary",) * 2, 48),
        name="attn_c",
    )(lam_rows, gsub, px, px, px, pc, pc)


D_TQ = 256
D_SPAN = D_TQ + 2 * WINDOW


def _attn_d_kernel(sink_ref, q_ref, k_ref, v_ref, kc_ref, vc_ref, o_ref, *, layer):
    kvh = pl.program_id(1)
    kc, vc = kc_ref[0], vc_ref[0]
    row = lax.broadcasted_iota(jnp.int32, (2 * D_TQ, D_SPAN), 0)
    rel = (row & (D_TQ - 1)) - lax.broadcasted_iota(jnp.int32, (2 * D_TQ, D_SPAN), 1)
    head_row = lax.broadcasted_iota(jnp.int32, (2 * D_TQ, 1), 0)
    sink = jnp.where(head_row < D_TQ, sink_ref[layer, kvh * 2], sink_ref[layer, kvh * 2 + 1]) * LOG2E
    for i in range(SEQ // D_TQ):
        q0 = i * D_TQ
        k0 = min(max(q0 - WINDOW, 0), SEQ - D_SPAN)
        kw = k_ref[0, k0:k0 + D_SPAN, :]
        vw = v_ref[0, k0:k0 + D_SPAN, :]
        valid = jnp.abs(rel + (q0 - k0)) <= WINDOW
        q = jnp.concatenate([q_ref[0, q0:q0 + D_TQ, :LANES], q_ref[0, q0:q0 + D_TQ, LANES:]], axis=0)
        sw = jnp.where(valid, _dot_t(q, kw), NEG_INF)
        sc = _dot_t(q, kc)
        m = jnp.maximum(_rowmax([sw, sc]), sink)
        ew = jnp.exp2(sw - m)
        ec = jnp.exp2(sc - m)
        den = _rowsum([ew, ec]) + jnp.exp2(sink - m)
        o = ((_dot(ew.astype(BF16), vw) + _dot(ec.astype(BF16), vc)) / den).astype(BF16)
        o_ref[0, q0:q0 + D_TQ, :LANES] = o[:D_TQ]
        o_ref[0, q0:q0 + D_TQ, LANES:] = o[D_TQ:]


def _attn_d_call(px, pc, ctx_tiles, sink, layer):
    b = px.shape[0]
    dq, dk, dv = SEG["dq"], SEG["dk"], SEG["dv"]
    dkc, dvc = _seg_block(ctx_tiles, "dk"), _seg_block(ctx_tiles, "dv")
    return pl.pallas_call(
        functools.partial(_attn_d_kernel, layer=layer),
        grid=(b, 2),
        in_specs=[
            pl.BlockSpec(memory_space=pltpu.SMEM),
            pl.BlockSpec((1, SEQ, 2 * LANES), lambda i, h: (i, 0, dq // 2 + h)),
            pl.BlockSpec((1, SEQ, LANES), lambda i, h: (i, 0, dk + h)),
            pl.BlockSpec((1, SEQ, LANES), lambda i, h: (i, 0, dv + h)),
            pl.BlockSpec((1, CTX_LEN, LANES), lambda i, h: (i, 0, dkc + h)),
            pl.BlockSpec((1, CTX_LEN, LANES), lambda i, h: (i, 0, dvc + h)),
        ],
        out_specs=pl.BlockSpec((1, SEQ, 2 * LANES), lambda i, h: (i, 0, h)),
        out_shape=jax.ShapeDtypeStruct((b, SEQ, GROUP_WIDTH), BF16),
        compiler_params=_params(("arbitrary",) * 2, 32),
        name="attn_d",
    )(sink, px, px, px, pc, pc)


def _softmax_pv(q, k, v, sink=None):
    s = _dot_t(q, k)
    m = jnp.max(s, axis=-1, keepdims=True)
    if sink is not None:
        m = jnp.maximum(m, sink)
    e = jnp.exp2(s - m)
    den = jnp.sum(e, axis=-1, keepdims=True)
    if sink is not None:
        den = den + jnp.exp2(sink - m)
    return _dot(e.astype(BF16), v) / den


def _ctx_attn_kernel(sink_ref, lam_ref, gsub_ref, p_ref, o_ref, *, layer, lam_init):
    def blk(name, h):
        c = (SEG[name] + h) * LANES
        return p_ref[0, :, c:c + LANES]

    def put(group, h, o):
        c = group * GROUP_WIDTH + h * LANES
        o_ref[0, :, c:c + LANES] = o.astype(BF16)

    lam = _diff_lambda(lam_ref, lam_init)
    for h in range(4):
        put(0, h, _softmax_pv(blk("aq", h), blk("ak", h // 2), blk("av", h // 2)))
        put(1, h, _softmax_pv(blk("bq", h), blk("bk", h), blk("bv", h)))
        put(2, h, _diff_head(blk("cq", h), [blk("ck", h)], [blk("cv", h)], lam, gsub_ref[...], lam_init))
        put(3, h, _softmax_pv(blk("dq", h), blk("dk", h // 2), blk("dv", h // 2),
                              sink=sink_ref[layer, h] * LOG2E))


def _ctx_attn_call(pc, sink, lam_rows, gsub, layer, lam_init):
    b = pc.shape[0]
    return pl.pallas_call(
        functools.partial(_ctx_attn_kernel, layer=layer, lam_init=lam_init),
        grid=(b,),
        in_specs=[
            pl.BlockSpec(memory_space=pltpu.SMEM),
            pl.BlockSpec((None, 8, LANES), lambda i: (layer, 0, 0)),
            pl.BlockSpec((None, 1, LANES), lambda i: (layer, 0, 0)),
            pl.BlockSpec((1, CTX_LEN, PROJ_WIDTH), lambda i: (i, 0, 0)),
        ],
        out_specs=pl.BlockSpec((1, CTX_LEN, D_MODEL), lambda i: (i, 0, 0)),
        out_shape=jax.ShapeDtypeStruct((b, CTX_LEN, D_MODEL), BF16),
        compiler_params=_params(("arbitrary",), 32),
        name="ctx_attn",
    )(sink, lam_rows, gsub, pc)


POST_TM = 512
POST_TH = 1024


def _post_kernel(x_ref, oa_ref, ob_ref, oc_ref, od_ref, wout_ref, gate_mix_ref, shift_ref, scale_ref,
                 gate_mlp_ref, g_ref, wup_ref, wdn_ref, gfin_ref, out_ref, x1_ref, h_ref, *, final):
    j = pl.program_id(1)

    @pl.when(j == 0)
    def _():
        mix = functools.reduce(jnp.add, [
            _dot(o[...], wout_ref[n * GROUP_WIDTH:(n + 1) * GROUP_WIDTH, :])
            for n, o in enumerate((oa_ref, ob_ref, oc_ref, od_ref))])
        x1 = x_ref[...] + gate_mix_ref[0] * mix
        x1_ref[...] = x1
        h_ref[...] = (_rms(x1, g_ref[...]) * (1.0 + scale_ref[0]) + shift_ref[0]).astype(BF16)
        out_ref[...] = jnp.zeros_like(out_ref)

    u = _dot(h_ref[...], wup_ref[...])
    act = jnp.square(jnp.maximum(u, 0.0)).astype(BF16)
    out_ref[...] += _dot(act, wdn_ref[...])

    @pl.when(j == pl.num_programs(1) - 1)
    def _():
        y = x1_ref[...] + gate_mlp_ref[0] * out_ref[...]
        if final:
            y = _rms(y, gfin_ref[...])
        out_ref[...] = y


def _post_call(x2d, mix_parts, mix_blocks, w_out, mod, g_mlp, w_up, w_down, layer, g_final, *,
               mod_row, final):
    t_tok = x2d.shape[0]
    tm, th = POST_TM, POST_TH
    once = pl.Buffered(1)

    def mod_spec(k):
        return pl.BlockSpec((None, 1, 1, D_MODEL), lambda i, j: (layer, mod_row(i) * N_MOD + k, 0, 0))

    def part_spec(blk):
        return pl.BlockSpec((tm, GROUP_WIDTH), lambda i, j: (i, blk))

    return pl.pallas_call(
        functools.partial(_post_kernel, final=final),
        grid=(t_tok // tm, MLP_HIDDEN // th),
        in_specs=[
            pl.BlockSpec((tm, D_MODEL), lambda i, j: (i, 0)),
            *[part_spec(blk) for blk in mix_blocks],
            pl.BlockSpec((None, D_MODEL, D_MODEL), lambda i, j: (layer, 0, 0), pipeline_mode=once),
            mod_spec(2), mod_spec(3), mod_spec(4), mod_spec(5),
            pl.BlockSpec((None, 1, D_MODEL), lambda i, j: (layer, 0, 0)),
            pl.BlockSpec((None, D_MODEL, th), lambda i, j: (layer, 0, j)),
            pl.BlockSpec((None, th, D_MODEL), lambda i, j: (layer, j, 0)),
            pl.BlockSpec((1, D_MODEL), lambda i, j: (0, 0)),
        ],
        out_specs=pl.BlockSpec((tm, D_MODEL), lambda i, j: (i, 0)),
        out_shape=jax.ShapeDtypeStruct((t_tok, D_MODEL), F32),
        scratch_shapes=[pltpu.VMEM((tm, D_MODEL), F32), pltpu.VMEM((tm, D_MODEL), BF16)],
        compiler_params=_params(("arbitrary", "arbitrary"), 60),
        name="post",
    )(x2d, *mix_parts, w_out, mod, mod, mod, mod, g_mlp, w_up, w_down, g_final)


def _rope_rows(n_tok, dim):
    t = jnp.arange(n_tok)
    row = (t // GRID_W).astype(F32)
    col = (t % GRID_W).astype(F32)
    n_freq = dim // 4
    inv_freq = ROPE_THETA ** (-jnp.arange(n_freq, dtype=F32) / n_freq)
    ang = jnp.concatenate([row[:, None] * inv_freq, col[:, None] * inv_freq], axis=-1)
    cos = jnp.repeat(jnp.cos(ang), 2, axis=-1)
    sin = jnp.repeat(jnp.sin(ang), 2, axis=-1)
    even = (jnp.arange(dim) % 2) == 0
    parts = [cos, jnp.where(even, -sin, 0.0), jnp.where(even, 0.0, sin)]
    return [jnp.tile(p, (1, LANES // dim)) for p in parts]


def _rope_table(n_tok):
    return jnp.concatenate(_rope_rows(n_tok, HEAD_DIM) + _rope_rows(n_tok, DIFF_QK_DIM), axis=-1)


def _identity_rope_table(n_tok):
    one = jnp.ones((n_tok, LANES), F32)
    zero = jnp.zeros((n_tok, LANES), F32)
    return jnp.concatenate([one, zero, zero] * 2, axis=-1)


def _pad_rows(rows, n_rows=8):
    depth = rows[0].shape[0]
    padded = [jnp.pad(r.astype(F32), ((0, 0), (0, LANES - r.shape[1]))) for r in rows]
    padded += [jnp.zeros((depth, LANES), F32)] * (n_rows - len(rows))
    return jnp.stack(padded, axis=1)


def kernel(x, c, ctx, c_ctx, g_mix, g_mlp, w_mod, b_mod, w_in, w_out, gqa_gq, gqa_gk, na_rpb,
           diff_lq1, diff_lk1, diff_lq2, diff_lk2, diff_gsub, swa_sink, w_up, w_down, g_final):
    b, s, d = x.shape
    assert (s, d) == (SEQ, D_MODEL) and ctx.shape == (b, CTX_LEN, D_MODEL) and b <= CTX_MOD_ROW

    c_rows = jnp.concatenate(
        [c, jnp.zeros((CTX_MOD_ROW - b, D_MODEL), F32), c_ctx[None],
         jnp.zeros((MOD_ROWS - CTX_MOD_ROW - 1, D_MODEL), F32)], axis=0)
    mod = _mod_call(c_rows, w_mod, b_mod).reshape(DEPTH, MOD_ROWS * N_MOD, 1, D_MODEL)

    w_in_b = w_in.astype(BF16)
    w_out_b = w_out.astype(BF16)
    w_up_b = w_up.astype(BF16)
    w_down_b = w_down.astype(BF16)

    rope_x = _rope_table(SEQ)
    rope_c = _identity_rope_table(PROJ_TM)
    x_tiles_per_seq = SEQ // PROJ_TM
    x_post_tiles_per_seq = SEQ // POST_TM
    g_final2 = g_final.reshape(1, D_MODEL)
    g_mix3 = g_mix.reshape(DEPTH, 1, D_MODEL)
    g_mlp3 = g_mlp.reshape(DEPTH, 1, D_MODEL)
    head_gains = _pad_rows([gqa_gq, gqa_gk])
    lam_rows = _pad_rows([diff_lq1, diff_lk1, diff_lq2, diff_lk2])
    gsub = diff_gsub.reshape(DEPTH, 1, HEAD_DIM)
    sink = swa_sink.astype(F32)
    nb_bias = _na_bias_table(na_rpb)

    x2d = x.reshape(b * s, D_MODEL)
    c2d = ctx.reshape(b * CTX_LEN, D_MODEL)
    for l in range(DEPTH):
        with_ctx = l < DEPTH - 1
        lam_init = 0.8 - 0.6 * math.exp(-0.3 * l)
        ctx_tiles = ALL_TILES if with_ctx else KV_TILES

        px = _proj_call(x2d, mod, g_mix3, w_in_b, l, rope_x, head_gains, tiles=ALL_TILES,
                        mod_row=lambda i: i // x_tiles_per_seq, rope_tile=lambda i: i % x_tiles_per_seq)
        pc = _proj_call(c2d, mod, g_mix3, w_in_b, l, rope_c, head_gains, tiles=ctx_tiles,
                        mod_row=lambda i: CTX_MOD_ROW, rope_tile=lambda i: 0)
        px = px.reshape(b, SEQ, PROJ_WIDTH)
        pc = pc.reshape(b, CTX_LEN, len(ctx_tiles) * COL_TILE)

        oa = _attn_a_call(px, pc, ctx_tiles)
        ob = _attn_b_call(px, pc, ctx_tiles, nb_bias, l)
        oc = _attn_c_call(px, pc, ctx_tiles, lam_rows, gsub, l, lam_init)
        od = _attn_d_call(px, pc, ctx_tiles, sink, l)
        parts = [o.reshape(b * s, GROUP_WIDTH) for o in (oa, ob, oc, od)]
        if with_ctx:
            o_ctx = _ctx_attn_call(pc, sink, lam_rows, gsub, l, lam_init).reshape(b * CTX_LEN, D_MODEL)

        x2d = _post_call(x2d, parts, (0, 0, 0, 0), w_out_b, mod, g_mlp3, w_up_b, w_down_b, l, g_final2,
                         mod_row=lambda i: i // x_post_tiles_per_seq, final=not with_ctx)
        if with_ctx:
            c2d = _post_call(c2d, [o_ctx] * 4, (0, 1, 2, 3), w_out_b, mod, g_mlp3, w_up_b, w_down_b, l,
                             g_final2, mod_row=lambda i: CTX_MOD_ROW, final=False)
    return x2d.reshape(b, s, D_MODEL)
```

```python
import functools
import math

import jax
import jax.numpy as jnp
from jax import lax
from jax.experimental import pallas as pl
from jax.experimental.pallas import tpu as pltpu

D_MODEL = 2048
SEQ = 2048
DEPTH = 2
GRID_W = 64
CTX_LEN = 256
HEAD_DIM = 128
GROUP_WIDTH = 512
WIN_ROWS = 8
WIN_COLS = 16
DIFF_QK_DIM = 64
WINDOW = 128
MLP_HIDDEN = 4 * D_MODEL
N_MOD = 6
ROPE_THETA = 10000.0
NORM_EPS = 1e-6
NEG_INF = -1e30
PROJ_WIDTH = 5120
LOG2E = math.log2(math.e)

LANES = 128
SUBLANES = 8
COL_TILE = 512
MOD_ROWS = 16
CTX_MOD_ROW = 8

F32 = jnp.float32
BF16 = jnp.bfloat16

SEG = dict(aq=0, ak=4, av=6, bq=8, bk=12, bv=16, cq=20, ck=24, cv=28, dq=32, dk=36, dv=38)
ALL_TILES = tuple(range(PROJ_WIDTH // COL_TILE))
KV_TILES = (1, 3, 4, 6, 7, 9)
Q_SCALE_H = HEAD_DIM ** -0.5 * LOG2E
Q_SCALE_D = DIFF_QK_DIM ** -0.5 * LOG2E
_ROPE_H, _ROPE_D = 0, 1
BLOCK_KIND = {}
for _b in range(PROJ_WIDTH // LANES):
    if _b < 4:
        BLOCK_KIND[_b] = (0, _ROPE_H, Q_SCALE_H)
    elif _b < 6:
        BLOCK_KIND[_b] = (1, _ROPE_H, None)
    elif 8 <= _b < 12:
        BLOCK_KIND[_b] = (None, None, Q_SCALE_H)
    elif 20 <= _b < 24:
        BLOCK_KIND[_b] = (None, _ROPE_D, Q_SCALE_D)
    elif 24 <= _b < 28:
        BLOCK_KIND[_b] = (None, _ROPE_D, None)
    elif 32 <= _b < 36:
        BLOCK_KIND[_b] = (None, _ROPE_H, Q_SCALE_H)
    elif 36 <= _b < 38:
        BLOCK_KIND[_b] = (None, _ROPE_H, None)
    else:
        BLOCK_KIND[_b] = (None, None, None)


def _seg_block(tiles, name):
    b = SEG[name]
    t, r = divmod(b, COL_TILE // LANES)
    return tiles.index(t) * (COL_TILE // LANES) + r


def _params(dims, vmem_mb, flags=None):
    return pltpu.CompilerParams(dimension_semantics=dims, vmem_limit_bytes=vmem_mb * 1024 * 1024, flags=flags)


def _rms(x, g):
    ms = jnp.mean(x * x, axis=-1, keepdims=True)
    return x * lax.rsqrt(ms + NORM_EPS) * g


def _dot(a, b):
    return jnp.dot(a, b, preferred_element_type=F32)


def _dot_t(a, b):
    return lax.dot_general(a, b, (((1,), (1,)), ((), ())), preferred_element_type=F32)


def _dot_tn(a, b):
    return lax.dot_general(a, b, (((0,), (0,)), ((), ())), preferred_element_type=F32)


def _fold8(x, reduce_fn):
    return reduce_fn(x.reshape(x.shape[0] // SUBLANES, SUBLANES, x.shape[1]), axis=0)


def _colmax(parts):
    return functools.reduce(jnp.maximum, [jnp.max(p, axis=0, keepdims=True) for p in parts])


def _colsum(parts):
    return functools.reduce(jnp.add, [jnp.sum(p, axis=0, keepdims=True) for p in parts])


def _rowmax(parts):
    return functools.reduce(jnp.maximum, [jnp.max(p, axis=-1, keepdims=True) for p in parts])


def _rowsum(parts):
    return functools.reduce(jnp.add, [jnp.sum(p, axis=-1, keepdims=True) for p in parts])


MOD_TN = 1024


def _mod_kernel(c_ref, w_ref, b_ref, o_ref):
    c = c_ref[...]
    cond = c * (1.0 / (1.0 + jnp.exp(-c)))
    o_ref[0] = _dot(cond.astype(BF16), w_ref[0].astype(BF16)) + b_ref[0]


def _mod_call(c_rows, w_mod, b_mod):
    n = N_MOD * D_MODEL
    return pl.pallas_call(
        _mod_kernel,
        grid=(DEPTH, n // MOD_TN),
        in_specs=[
            pl.BlockSpec((MOD_ROWS, D_MODEL), lambda l, j: (0, 0)),
            pl.BlockSpec((1, D_MODEL, MOD_TN), lambda l, j: (l, 0, j)),
            pl.BlockSpec((1, 1, MOD_TN), lambda l, j: (l, 0, j)),
        ],
        out_specs=pl.BlockSpec((1, MOD_ROWS, MOD_TN), lambda l, j: (l, 0, j)),
        out_shape=jax.ShapeDtypeStruct((DEPTH, MOD_ROWS, n), F32),
        compiler_params=_params(("arbitrary", "arbitrary"), 40),
        name="mod",
    )(c_rows, w_mod, b_mod.reshape(DEPTH, 1, n))


PROJ_TM = 512


def _proj_kernel(x_ref, shift_ref, scale_ref, g_ref, w_ref, rope_ref, hg_ref, o_ref, *, tiles):
    h = _rms(x_ref[...], g_ref[...]) * (1.0 + scale_ref[0]) + shift_ref[0]
    hb = h.astype(BF16)
    per = COL_TILE // LANES
    for n, t in enumerate(tiles):
        y = _dot(hb, w_ref[:, t * COL_TILE:(t + 1) * COL_TILE])
        for j in range(per):
            gain_row, rope, q_scale = BLOCK_KIND[t * per + j]
            yj = y[:, j * LANES:(j + 1) * LANES]
            if gain_row is not None:
                yj = _rms(yj, hg_ref[gain_row:gain_row + 1, :])
            if rope is not None:
                base = rope * 3 * LANES
                cos = rope_ref[:, base:base + LANES]
                sin_next = rope_ref[:, base + LANES:base + 2 * LANES]
                sin_prev = rope_ref[:, base + 2 * LANES:base + 3 * LANES]
                yj = (yj * cos + pltpu.roll(yj, LANES - 1, 1) * sin_next
                      + pltpu.roll(yj, 1, 1) * sin_prev)
            if q_scale is not None:
                yj = yj * q_scale
            o_ref[:, (n * per + j) * LANES:(n * per + j + 1) * LANES] = yj.astype(BF16)


def _proj_call(x2d, mod, g, w_in, layer, rope, head_gains, *, tiles, mod_row, rope_tile):
    t_tok = x2d.shape[0]
    tm = PROJ_TM
    wout = len(tiles) * COL_TILE
    return pl.pallas_call(
        functools.partial(_proj_kernel, tiles=tiles),
        grid=(t_tok // tm,),
        in_specs=[
            pl.BlockSpec((tm, D_MODEL), lambda i: (i, 0)),
            pl.BlockSpec((None, 1, 1, D_MODEL), lambda i: (layer, mod_row(i) * N_MOD + 0, 0, 0)),
            pl.BlockSpec((None, 1, 1, D_MODEL), lambda i: (layer, mod_row(i) * N_MOD + 1, 0, 0)),
            pl.BlockSpec((None, 1, D_MODEL), lambda i: (layer, 0, 0)),
            pl.BlockSpec((None, D_MODEL, PROJ_WIDTH), lambda i: (layer, 0, 0), pipeline_mode=pl.Buffered(1)),
            pl.BlockSpec((tm, 6 * LANES), lambda i: (rope_tile(i), 0)),
            pl.BlockSpec((None, 8, LANES), lambda i: (layer, 0, 0)),
        ],
        out_specs=pl.BlockSpec((tm, wout), lambda i: (i, 0)),
        out_shape=jax.ShapeDtypeStruct((t_tok, wout), BF16),
        compiler_params=_params(("arbitrary",), 56),
        name="proj",
    )(x2d, mod, mod, g, w_in, rope, head_gains)


A_TQ = 512
KEY_CHUNK = 256


def _attn_a_kernel(q_ref, k_ref, v_ref, kc_ref, vc_ref, o_ref, kall_ref, vall_ref, s_ref, p_ref):
    n_keys = SEQ + CTX_LEN
    kall_ref[:SEQ, :] = k_ref[0]
    kall_ref[SEQ:, :] = kc_ref[0]
    vall_ref[:SEQ, :] = v_ref[0]
    vall_ref[SEQ:, :] = vc_ref[0]
    units = [(r0, g) for r0 in range(0, SEQ, A_TQ) for g in range(2)]

    def scores(n):
        r0, g = units[n]
        s_ref[n % 2] = _dot_t(kall_ref[...], q_ref[0, r0:r0 + A_TQ, g * LANES:(g + 1) * LANES])

    scores(0)
    for n, (r0, g) in enumerate(units):
        if n + 1 < len(units):
            scores(n + 1)
        slot = n % 2
        chunks = range(0, n_keys, KEY_CHUNK)
        m8 = functools.reduce(jnp.maximum, [_fold8(s_ref[slot, c0:c0 + KEY_CHUNK, :], jnp.max) for c0 in chunks])
        m = jnp.max(m8, axis=0, keepdims=True)
        den8 = jnp.zeros((SUBLANES, A_TQ), F32)
        for c0 in chunks:
            e = jnp.exp2(s_ref[slot, c0:c0 + KEY_CHUNK, :] - m)
            den8 = den8 + _fold8(e, jnp.sum)
            p_ref[c0:c0 + KEY_CHUNK, :] = e.astype(BF16)
        den = jnp.sum(den8, axis=0, keepdims=True)
        o_t = _dot_tn(vall_ref[...], p_ref[...])
        o_ref[0, r0:r0 + A_TQ, g * LANES:(g + 1) * LANES] = (o_t / den).T.astype(BF16)


def _attn_a_call(px, pc, ctx_tiles):
    b = px.shape[0]
    ak, av = SEG["ak"], SEG["av"]
    akc, avc = _seg_block(ctx_tiles, "ak"), _seg_block(ctx_tiles, "av")
    return pl.pallas_call(
        _attn_a_kernel,
        grid=(b, 2),
        in_specs=[
            pl.BlockSpec((1, SEQ, 2 * LANES), lambda i, h: (i, 0, h)),
            pl.BlockSpec((1, SEQ, LANES), lambda i, h: (i, 0, ak + h)),
            pl.BlockSpec((1, SEQ, LANES), lambda i, h: (i, 0, av + h)),
            pl.BlockSpec((1, CTX_LEN, LANES), lambda i, h: (i, 0, akc + h)),
            pl.BlockSpec((1, CTX_LEN, LANES), lambda i, h: (i, 0, avc + h)),
        ],
        out_specs=pl.BlockSpec((1, SEQ, 2 * LANES), lambda i, h: (i, 0, h)),
        out_shape=jax.ShapeDtypeStruct((b, SEQ, GROUP_WIDTH), BF16),
        scratch_shapes=[pltpu.VMEM((SEQ + CTX_LEN, LANES), BF16), pltpu.VMEM((SEQ + CTX_LEN, LANES), BF16),
                        pltpu.VMEM((2, SEQ + CTX_LEN, A_TQ), F32), pltpu.VMEM((SEQ + CTX_LEN, A_TQ), BF16)],
        compiler_params=_params(("arbitrary",) * 2, 48),
        name="attn_a",
    )(px, px, px, pc, pc)


NB_ROWS = SEQ // GRID_W
NB_QROWS = 4
NB_KROWS = NB_QROWS + WIN_ROWS - 1
NB_BLOCKS = NB_ROWS // NB_QROWS
NB_TQ = NB_QROWS * GRID_W
NB_KEYS = NB_KROWS * GRID_W


def _nb_window_start(j):
    return min(max(j * NB_QROWS - WIN_ROWS // 2, 0), NB_ROWS - NB_KROWS)


def _nb_class(j):
    return 0 if j == 0 else (2 if j == NB_BLOCKS - 1 else 1)


def _na_bias_table(rpb):
    n_dr, n_dc = rpb.shape[-2:]
    rpb = rpb.astype(F32).reshape(-1, n_dr, n_dc)
    n_tab = rpb.shape[0]
    period = 2 * GRID_W
    fill = jnp.full((n_tab, n_dr, period - n_dc), NEG_INF, F32)
    ext = jnp.concatenate([rpb[..., WIN_COLS - 1:], fill, rpb[..., :WIN_COLS - 1]], axis=-1)
    flat = jnp.tile(ext, (1, 1, GRID_W))[..., :GRID_W * (period - 1)]
    toep = flat.reshape(n_tab, n_dr, GRID_W, period - 1)[..., :GRID_W]
    c = jnp.arange(GRID_W)
    c_start = jnp.clip(c - WIN_COLS // 2, 0, GRID_W - WIN_COLS)
    kc = jnp.arange(GRID_W)
    in_win = (kc[None, :] >= c_start[:, None]) & (kc[None, :] < c_start[:, None] + WIN_COLS)
    toep = jnp.where(in_win, toep * LOG2E, NEG_INF)

    def masked_rows(n):
        return jnp.full((n_tab, n, GRID_W, GRID_W), NEG_INF, F32)

    classes = []
    for j in (0, 1, NB_BLOCKS - 1):
        q_rows = []
        for rq in range(NB_QROWS):
            r = j * NB_QROWS + rq
            r_start = min(max(r - WIN_ROWS // 2, 0), NB_ROWS - WIN_ROWS)
            lo = r_start - _nb_window_start(j)
            drow0 = r_start - r + WIN_ROWS - 1
            pieces = [masked_rows(lo), toep[:, drow0:drow0 + WIN_ROWS], masked_rows(NB_KROWS - WIN_ROWS - lo)]
            q_rows.append(jnp.concatenate([p for p in pieces if p.shape[1]], axis=1))
        classes.append(jnp.stack(q_rows, axis=1))
    table = jnp.stack(classes, axis=1).transpose(0, 1, 2, 4, 3, 5)
    return table.reshape(n_tab, 3, NB_TQ, NB_KEYS)


def _attn_b_kernel(q_ref, k_ref, v_ref, kc_ref, vc_ref, bias_ref, o_ref):
    kc, vc = kc_ref[0], vc_ref[0]
    for j in range(NB_BLOCKS):
        q0 = j * NB_TQ
        k0 = _nb_window_start(j) * GRID_W
        q = q_ref[0, q0:q0 + NB_TQ, :]
        kw = k_ref[0, k0:k0 + NB_KEYS, :]
        vw = v_ref[0, k0:k0 + NB_KEYS, :]
        sn = _dot_t(q, kw) + bias_ref[0, _nb_class(j)]
        sc = _dot_t(q, kc)
        m = _rowmax([sn, sc])
        en = jnp.exp2(sn - m)
        ec = jnp.exp2(sc - m)
        den = _rowsum([en, ec])
        o = _dot(en.astype(BF16), vw) + _dot(ec.astype(BF16), vc)
        o_ref[0, q0:q0 + NB_TQ, :] = (o / den).astype(BF16)


def _attn_b_call(px, pc, ctx_tiles, bias, layer):
    b = px.shape[0]
    bq, bk, bv = SEG["bq"], SEG["bk"], SEG["bv"]
    bkc, bvc = _seg_block(ctx_tiles, "bk"), _seg_block(ctx_tiles, "bv")
    return pl.pallas_call(
        _attn_b_kernel,
        grid=(b, 4),
        in_specs=[
            pl.BlockSpec((1, SEQ, LANES), lambda i, h: (i, 0, bq + h)),
            pl.BlockSpec((1, SEQ, LANES), lambda i, h: (i, 0, bk + h)),
            pl.BlockSpec((1, SEQ, LANES), lambda i, h: (i, 0, bv + h)),
            pl.BlockSpec((1, CTX_LEN, LANES), lambda i, h: (i, 0, bkc + h)),
            pl.BlockSpec((1, CTX_LEN, LANES), lambda i, h: (i, 0, bvc + h)),
            pl.BlockSpec((1, 3, NB_TQ, NB_KEYS), lambda i, h: (layer * 4 + h, 0, 0, 0)),
        ],
        out_specs=pl.BlockSpec((1, SEQ, LANES), lambda i, h: (i, 0, h)),
        out_shape=jax.ShapeDtypeStruct((b, SEQ, GROUP_WIDTH), BF16),
        compiler_params=_params(("arbitrary",) * 2, 32),
        name="attn_b",
    )(px, px, px, pc, pc, bias)


C_TQ = 512


def _diff_lambda(lam_ref, lam_init):
    a = jnp.sum(lam_ref[0:1, :] * lam_ref[1:2, :], axis=-1, keepdims=True)
    b = jnp.sum(lam_ref[2:3, :] * lam_ref[3:4, :], axis=-1, keepdims=True)
    return jnp.exp(a) - jnp.exp(b) + lam_init


def _diff_head(q, keys, vals, lam, gsub, lam_init):
    lane = lax.broadcasted_iota(jnp.int32, q.shape, 1)
    zero = jnp.zeros_like(q)
    maps = []
    for sub in range(2):
        qs = jnp.where((lane < DIFF_QK_DIM) == (sub == 0), q, zero)
        s = [_dot_t(k, qs) for k in keys]
        m = _colmax(s)
        e = [jnp.exp2(x - m) for x in s]
        maps.append((e, _colsum(e)))
    (e1, den1), (e2, den2) = maps
    ratio = lam * den1 / den2
    o_t = functools.reduce(jnp.add, [
        _dot_tn(v, (a - ratio * b).astype(BF16)) for a, b, v in zip(e1, e2, vals)]) / den1
    return _rms(o_t.T, gsub) * (1.0 - lam_init)


def _attn_c_kernel(lam_ref, gsub_ref, q_ref, k_ref, v_ref, kc_ref, vc_ref, o_ref,
                   kall_ref, vall_ref, s_ref, p_ref, *, lam_init):
    n_keys = SEQ + CTX_LEN
    kall_ref[:SEQ, :] = k_ref[0]
    kall_ref[SEQ:, :] = kc_ref[0]
    vall_ref[:SEQ, :] = v_ref[0]
    vall_ref[SEQ:, :] = vc_ref[0]
    lam = _diff_lambda(lam_ref, lam_init)
    lane = lax.broadcasted_iota(jnp.int32, (C_TQ, LANES), 1)
    tiles = list(range(0, SEQ, C_TQ))
    chunks = range(0, n_keys, KEY_CHUNK)

    def scores(n):
        q = q_ref[0, tiles[n]:tiles[n] + C_TQ, :]
        for sub in range(2):
            qs = jnp.where((lane < DIFF_QK_DIM) == (sub == 0), q, jnp.zeros_like(q))
            s_ref[n % 2, sub] = _dot_t(kall_ref[...], qs)

    scores(0)
    for n, r0 in enumerate(tiles):
        if n + 1 < len(tiles):
            scores(n + 1)
        slot = n % 2
        dens = []
        for sub in range(2):
            m8 = functools.reduce(jnp.maximum,
                                  [_fold8(s_ref[slot, sub, c0:c0 + KEY_CHUNK, :], jnp.max) for c0 in chunks])
            m = jnp.max(m8, axis=0, keepdims=True)
            den8 = jnp.zeros((SUBLANES, C_TQ), F32)
            for c0 in chunks:
                e = jnp.exp2(s_ref[slot, sub, c0:c0 + KEY_CHUNK, :] - m)
                den8 = den8 + _fold8(e, jnp.sum)
                s_ref[slot, sub, c0:c0 + KEY_CHUNK, :] = e
            dens.append(jnp.sum(den8, axis=0, keepdims=True))
        den1, den2 = dens
        ratio = lam * den1 / den2
        for c0 in chunks:
            p_ref[c0:c0 + KEY_CHUNK, :] = (s_ref[slot, 0, c0:c0 + KEY_CHUNK, :]
                                           - ratio * s_ref[slot, 1, c0:c0 + KEY_CHUNK, :]).astype(BF16)
        o_t = _dot_tn(vall_ref[...], p_ref[...]) / den1
        o = _rms(o_t.T, gsub_ref[...]) * (1.0 - lam_init)
        o_ref[0, r0:r0 + C_TQ, :] = o.astype(BF16)


def _attn_c_call(px, pc, ctx_tiles, lam_rows, gsub, layer, lam_init):
    b = px.shape[0]
    cq, ck, cv = SEG["cq"], SEG["ck"], SEG["cv"]
    ckc, cvc = _seg_block(ctx_tiles, "ck"), _seg_block(ctx_tiles, "cv")
    return pl.pallas_call(
        functools.partial(_attn_c_kernel, lam_init=lam_init),
        grid=(b, 4),
        in_specs=[
            pl.BlockSpec((None, 8, LANES), lambda i, h: (layer, 0, 0)),
            pl.BlockSpec((None, 1, LANES), lambda i, h: (layer, 0, 0)),
            pl.BlockSpec((1, SEQ, LANES), lambda i, h: (i, 0, cq + h)),
            pl.BlockSpec((1, SEQ, LANES), lambda i, h: (i, 0, ck + h)),
            pl.BlockSpec((1, SEQ, LANES), lambda i, h: (i, 0, cv + h)),
            pl.BlockSpec((1, CTX_LEN, LANES), lambda i, h: (i, 0, ckc + h)),
            pl.BlockSpec((1, CTX_LEN, LANES), lambda i, h: (i, 0, cvc + h)),
        ],
        out_specs=pl.BlockSpec((1, SEQ, LANES), lambda i, h: (i, 0, h)),
        out_shape=jax.ShapeDtypeStruct((b, SEQ, GROUP_WIDTH), BF16),
        scratch_shapes=[pltpu.VMEM((SEQ + CTX_LEN, LANES), BF16), pltpu.VMEM((SEQ + CTX_LEN, LANES), BF16),
                        pltpu.VMEM((2, 2, SEQ + CTX_LEN, C_TQ), F32), pltpu.VMEM((SEQ + CTX_LEN, C_TQ), BF16)],
        compiler_params=_params(("arbitrary",) * 2, 48),
        name="attn_c",
    )(lam_rows, gsub, px, px, px, pc, pc)


D_TQ = 256
D_SPAN = D_TQ + 2 * WINDOW


def _attn_d_kernel(sink_ref, q_ref, k_ref, v_ref, kc_ref, vc_ref, o_ref, *, layer):
    kvh = pl.program_id(1)
    kc, vc = kc_ref[0], vc_ref[0]
    row = lax.broadcasted_iota(jnp.int32, (2 * D_TQ, D_SPAN), 0)
    rel = (row & (D_TQ - 1)) - lax.broadcasted_iota(jnp.int32, (2 * D_TQ, D_SPAN), 1)
    head_row = lax.broadcasted_iota(jnp.int32, (2 * D_TQ, 1), 0)
    sink = jnp.where(head_row < D_TQ, sink_ref[layer, kvh * 2], sink_ref[layer, kvh * 2 + 1]) * LOG2E
    for i in range(SEQ // D_TQ):
        q0 = i * D_TQ
        k0 = min(max(q0 - WINDOW, 0), SEQ - D_SPAN)
        kw = k_ref[0, k0:k0 + D_SPAN, :]
        vw = v_ref[0, k0:k0 + D_SPAN, :]
        valid = jnp.abs(rel + (q0 - k0)) <= WINDOW
        q = jnp.concatenate([q_ref[0, q0:q0 + D_TQ, :LANES], q_ref[0, q0:q0 + D_TQ, LANES:]], axis=0)
        sw = jnp.where(valid, _dot_t(q, kw), NEG_INF)
        sc = _dot_t(q, kc)
        m = jnp.maximum(_rowmax([sw, sc]), sink)
        ew = jnp.exp2(sw - m)
        ec = jnp.exp2(sc - m)
        den = _rowsum([ew, ec]) + jnp.exp2(sink - m)
        o = ((_dot(ew.astype(BF16), vw) + _dot(ec.astype(BF16), vc)) / den).astype(BF16)
        o_ref[0, q0:q0 + D_TQ, :LANES] = o[:D_TQ]
        o_ref[0, q0:q0 + D_TQ, LANES:] = o[D_TQ:]


def _attn_d_call(px, pc, ctx_tiles, sink, layer):
    b = px.shape[0]
    dq, dk, dv = SEG["dq"], SEG["dk"], SEG["dv"]
    dkc, dvc = _seg_block(ctx_tiles, "dk"), _seg_block(ctx_tiles, "dv")
    return pl.pallas_call(
        functools.partial(_attn_d_kernel, layer=layer),
        grid=(b, 2),
        in_specs=[
            pl.BlockSpec(memory_space=pltpu.SMEM),
            pl.BlockSpec((1, SEQ, 2 * LANES), lambda i, h: (i, 0, dq // 2 + h)),
            pl.BlockSpec((1, SEQ, LANES), lambda i, h: (i, 0, dk + h)),
            pl.BlockSpec((1, SEQ, LANES), lambda i, h: (i, 0, dv + h)),
            pl.BlockSpec((1, CTX_LEN, LANES), lambda i, h: (i, 0, dkc + h)),
            pl.BlockSpec((1, CTX_LEN, LANES), lambda i, h: (i, 0, dvc + h)),
        ],
        out_specs=pl.BlockSpec((1, SEQ, 2 * LANES), lambda i, h: (i, 0, h)),
        out_shape=jax.ShapeDtypeStruct((b, SEQ, GROUP_WIDTH), BF16),
        compiler_params=_params(("arbitrary",) * 2, 32),
        name="attn_d",
    )(sink, px, px, px, pc, pc)


def _softmax_pv(q, k, v, sink=None):
    s = _dot_t(q, k)
    m = jnp.max(s, axis=-1, keepdims=True)
    if sink is not None:
        m = jnp.maximum(m, sink)
    e = jnp.exp2(s - m)
    den = jnp.sum(e, axis=-1, keepdims=True)
    if sink is not None:
        den = den + jnp.exp2(sink - m)
    return _dot(e.astype(BF16), v) / den


def _ctx_attn_kernel(sink_ref, lam_ref, gsub_ref, p_ref, o_ref, *, layer, lam_init):
    def blk(name, h):
        c = (SEG[name] + h) * LANES
        return p_ref[0, :, c:c + LANES]

    def put(group, h, o):
        c = group * GROUP_WIDTH + h * LANES
        o_ref[0, :, c:c + LANES] = o.astype(BF16)

    lam = _diff_lambda(lam_ref, lam_init)
    for h in range(4):
        put(0, h, _softmax_pv(blk("aq", h), blk("ak", h // 2), blk("av", h // 2)))
        put(1, h, _softmax_pv(blk("bq", h), blk("bk", h), blk("bv", h)))
        put(2, h, _diff_head(blk("cq", h), [blk("ck", h)], [blk("cv", h)], lam, gsub_ref[...], lam_init))
        put(3, h, _softmax_pv(blk("dq", h), blk("dk", h // 2), blk("dv", h // 2),
                              sink=sink_ref[layer, h] * LOG2E))


def _ctx_attn_call(pc, sink, lam_rows, gsub, layer, lam_init):
    b = pc.shape[0]
    return pl.pallas_call(
        functools.partial(_ctx_attn_kernel, layer=layer, lam_init=lam_init),
        grid=(b,),
        in_specs=[
            pl.BlockSpec(memory_space=pltpu.SMEM),
            pl.BlockSpec((None, 8, LANES), lambda i: (layer, 0, 0)),
            pl.BlockSpec((None, 1, LANES), lambda i: (layer, 0, 0)),
            pl.BlockSpec((1, CTX_LEN, PROJ_WIDTH), lambda i: (i, 0, 0)),
        ],
        out_specs=pl.BlockSpec((1, CTX_LEN, D_MODEL), lambda i: (i, 0, 0)),
        out_shape=jax.ShapeDtypeStruct((b, CTX_LEN, D_MODEL), BF16),
        compiler_params=_params(("arbitrary",), 32),
        name="ctx_attn",
    )(sink, lam_rows, gsub, pc)


POST_TM = 512
POST_TH = 1024


def _post_kernel(x_ref, oa_ref, ob_ref, oc_ref, od_ref, wout_ref, gate_mix_ref, shift_ref, scale_ref,
                 gate_mlp_ref, g_ref, wup_ref, wdn_ref, gfin_ref, out_ref, x1_ref, h_ref, *, final):
    j = pl.program_id(1)

    @pl.when(j == 0)
    def _():
        mix = functools.reduce(jnp.add, [
            _dot(o[...], wout_ref[n * GROUP_WIDTH:(n + 1) * GROUP_WIDTH, :])
            for n, o in enumerate((oa_ref, ob_ref, oc_ref, od_ref))])
        x1 = x_ref[...] + gate_mix_ref[0] * mix
        x1_ref[...] = x1
        h_ref[...] = (_rms(x1, g_ref[...]) * (1.0 + scale_ref[0]) + shift_ref[0]).astype(BF16)
        out_ref[...] = jnp.zeros_like(out_ref)

    u = _dot(h_ref[...], wup_ref[...])
    act = jnp.square(jnp.maximum(u, 0.0)).astype(BF16)
    out_ref[...] += _dot(act, wdn_ref[...])

    @pl.when(j == pl.num_programs(1) - 1)
    def _():
        y = x1_ref[...] + gate_mlp_ref[0] * out_ref[...]
        if final:
            y = _rms(y, gfin_ref[...])
        out_ref[...] = y


def _post_call(x2d, mix_parts, mix_blocks, w_out, mod, g_mlp, w_up, w_down, layer, g_final, *,
               mod_row, final):
    t_tok = x2d.shape[0]
    tm, th = POST_TM, POST_TH
    once = pl.Buffered(1)

    def mod_spec(k):
        return pl.BlockSpec((None, 1, 1, D_MODEL), lambda i, j: (layer, mod_row(i) * N_MOD + k, 0, 0))

    def part_spec(blk):
        return pl.BlockSpec((tm, GROUP_WIDTH), lambda i, j: (i, blk))

    return pl.pallas_call(
        functools.partial(_post_kernel, final=final),
        grid=(t_tok // tm, MLP_HIDDEN // th),
        in_specs=[
            pl.BlockSpec((tm, D_MODEL), lambda i, j: (i, 0)),
            *[part_spec(blk) for blk in mix_blocks],
            pl.BlockSpec((None, D_MODEL, D_MODEL), lambda i, j: (layer, 0, 0), pipeline_mode=once),
            mod_spec(2), mod_spec(3), mod_spec(4), mod_spec(5),
            pl.BlockSpec((None, 1, D_MODEL), lambda i, j: (layer, 0, 0)),
            pl.BlockSpec((None, D_MODEL, th), lambda i, j: (layer, 0, j)),
            pl.BlockSpec((None, th, D_MODEL), lambda i, j: (layer, j, 0)),
            pl.BlockSpec((1, D_MODEL), lambda i, j: (0, 0)),
        ],
        out_specs=pl.BlockSpec((tm, D_MODEL), lambda i, j: (i, 0)),
        out_shape=jax.ShapeDtypeStruct((t_tok, D_MODEL), F32),
        scratch_shapes=[pltpu.VMEM((tm, D_MODEL), F32), pltpu.VMEM((tm, D_MODEL), BF16)],
        compiler_params=_params(("arbitrary", "arbitrary"), 60),
        name="post",
    )(x2d, *mix_parts, w_out, mod, mod, mod, mod, g_mlp, w_up, w_down, g_final)


def _rope_rows(n_tok, dim):
    t = jnp.arange(n_tok)
    row = (t // GRID_W).astype(F32)
    col = (t % GRID_W).astype(F32)
    n_freq = dim // 4
    inv_freq = ROPE_THETA ** (-jnp.arange(n_freq, dtype=F32) / n_freq)
    ang = jnp.concatenate([row[:, None] * inv_freq, col[:, None] * inv_freq], axis=-1)
    cos = jnp.repeat(jnp.cos(ang), 2, axis=-1)
    sin = jnp.repeat(jnp.sin(ang), 2, axis=-1)
    even = (jnp.arange(dim) % 2) == 0
    parts = [cos, jnp.where(even, -sin, 0.0), jnp.where(even, 0.0, sin)]
    return [jnp.tile(p, (1, LANES // dim)) for p in parts]


def _rope_table(n_tok):
    return jnp.concatenate(_rope_rows(n_tok, HEAD_DIM) + _rope_rows(n_tok, DIFF_QK_DIM), axis=-1)


def _identity_rope_table(n_tok):
    one = jnp.ones((n_tok, LANES), F32)
    zero = jnp.zeros((n_tok, LANES), F32)
    return jnp.concatenate([one, zero, zero] * 2, axis=-1)


def _pad_rows(rows, n_rows=8):
    depth = rows[0].shape[0]
    padded = [jnp.pad(r.astype(F32), ((0, 0), (0, LANES - r.shape[1]))) for r in rows]
    padded += [jnp.zeros((depth, LANES), F32)] * (n_rows - len(rows))
    return jnp.stack(padded, axis=1)


def kernel(x, c, ctx, c_ctx, g_mix, g_mlp, w_mod, b_mod, w_in, w_out, gqa_gq, gqa_gk, na_rpb,
           diff_lq1, diff_lk1, diff_lq2, diff_lk2, diff_gsub, swa_sink, w_up, w_down, g_final):
    b, s, d = x.shape
    assert (s, d) == (SEQ, D_MODEL) and ctx.shape == (b, CTX_LEN, D_MODEL) and b <= CTX_MOD_ROW

    c_rows = jnp.concatenate(
        [c, jnp.zeros((CTX_MOD_ROW - b, D_MODEL), F32), c_ctx[None],
         jnp.zeros((MOD_ROWS - CTX_MOD_ROW - 1, D_MODEL), F32)], axis=0)
    mod = _mod_call(c_rows, w_mod, b_mod).reshape(DEPTH, MOD_ROWS * N_MOD, 1, D_MODEL)

    w_in_b = w_in.astype(BF16)
    w_out_b = w_out.astype(BF16)
    w_up_b = w_up.astype(BF16)
    w_down_b = w_down.astype(BF16)

    rope_x = _rope_table(SEQ)
    rope_c = _identity_rope_table(PROJ_TM)
    x_tiles_per_seq = SEQ // PROJ_TM
    x_post_tiles_per_seq = SEQ // POST_TM
    g_final2 = g_final.reshape(1, D_MODEL)
    g_mix3 = g_mix.reshape(DEPTH, 1, D_MODEL)
    g_mlp3 = g_mlp.reshape(DEPTH, 1, D_MODEL)
    head_gains = _pad_rows([gqa_gq, gqa_gk])
    lam_rows = _pad_rows([diff_lq1, diff_lk1, diff_lq2, diff_lk2])
    gsub = diff_gsub.reshape(DEPTH, 1, HEAD_DIM)
    sink = swa_sink.astype(F32)
    nb_bias = _na_bias_table(na_rpb)

    x2d = x.reshape(b * s, D_MODEL)
    c2d = ctx.reshape(b * CTX_LEN, D_MODEL)
    for l in range(DEPTH):
        with_ctx = l < DEPTH - 1
        lam_init = 0.8 - 0.6 * math.exp(-0.3 * l)
        ctx_tiles = ALL_TILES if with_ctx else KV_TILES

        px = _proj_call(x2d, mod, g_mix3, w_in_b, l, rope_x, head_gains, tiles=ALL_TILES,
                        mod_row=lambda i: i // x_tiles_per_seq, rope_tile=lambda i: i % x_tiles_per_seq)
        pc = _proj_call(c2d, mod, g_mix3, w_in_b, l, rope_c, head_gains, tiles=ctx_tiles,
                        mod_row=lambda i: CTX_MOD_ROW, rope_tile=lambda i: 0)
        px = px.reshape(b, SEQ, PROJ_WIDTH)
        pc = pc.reshape(b, CTX_LEN, len(ctx_tiles) * COL_TILE)

        oa = _attn_a_call(px, pc, ctx_tiles)
        ob = _attn_b_call(px, pc, ctx_tiles, nb_bias, l)
        oc = _attn_c_call(px, pc, ctx_tiles, lam_rows, gsub, l, lam_init)
        od = _attn_d_call(px, pc, ctx_tiles, sink, l)
        parts = [o.reshape(b * s, GROUP_WIDTH) for o in (oa, ob, oc, od)]
        if with_ctx:
            o_ctx = _ctx_attn_call(pc, sink, lam_rows, gsub, l, lam_init).reshape(b * CTX_LEN, D_MODEL)

        x2d = _post_call(x2d, parts, (0, 0, 0, 0), w_out_b, mod, g_mlp3, w_up_b, w_down_b, l, g_final2,
                         mod_row=lambda i: i // x_post_tiles_per_seq, final=not with_ctx)
        if with_ctx:
            c2d = _post_call(c2d, [o_ctx] * 4, (0, 1, 2, 3), w_out_b, mod, g_mlp3, w_up_b, w_down_b, l,
                             g_final2, mod_row=lambda i: CTX_MOD_ROW, final=False)
    return x2d.reshape(b, s, D_MODEL)
```

```python
import functools
import math

import jax
import jax.numpy as jnp
from jax import lax
from jax.experimental import pallas as pl
from jax.experimental.pallas import tpu as pltpu

D_MODEL = 2048
SEQ = 2048
DEPTH = 2
GRID_W = 64
CTX_LEN = 256
HEAD_DIM = 128
GROUP_WIDTH = 512
WIN_ROWS = 8
WIN_COLS = 16
DIFF_QK_DIM = 64
WINDOW = 128
MLP_HIDDEN = 4 * D_MODEL
N_MOD = 6
ROPE_THETA = 10000.0
NORM_EPS = 1e-6
NEG_INF = -1e30
PROJ_WIDTH = 5120
LOG2E = math.log2(math.e)

LANES = 128
SUBLANES = 8
COL_TILE = 512
MOD_ROWS = 16
CTX_MOD_ROW = 8

F32 = jnp.float32
BF16 = jnp.bfloat16

SEG = dict(aq=0, ak=4, av=6, bq=8, bk=12, bv=16, cq=20, ck=24, cv=28, dq=32, dk=36, dv=38)
ALL_TILES = tuple(range(PROJ_WIDTH // COL_TILE))
KV_TILES = (1, 3, 4, 6, 7, 9)
Q_SCALE_H = HEAD_DIM ** -0.5 * LOG2E
Q_SCALE_D = DIFF_QK_DIM ** -0.5 * LOG2E
_ROPE_H, _ROPE_D = 0, 1
BLOCK_KIND = {}
for _b in range(PROJ_WIDTH // LANES):
    if _b < 4:
        BLOCK_KIND[_b] = (0, _ROPE_H, Q_SCALE_H)
    elif _b < 6:
        BLOCK_KIND[_b] = (1, _ROPE_H, None)
    elif 8 <= _b < 12:
        BLOCK_KIND[_b] = (None, None, Q_SCALE_H)
    elif 20 <= _b < 24:
        BLOCK_KIND[_b] = (None, _ROPE_D, Q_SCALE_D)
    elif 24 <= _b < 28:
        BLOCK_KIND[_b] = (None, _ROPE_D, None)
    elif 32 <= _b < 36:
        BLOCK_KIND[_b] = (None, _ROPE_H, Q_SCALE_H)
    elif 36 <= _b < 38:
        BLOCK_KIND[_b] = (None, _ROPE_H, None)
    else:
        BLOCK_KIND[_b] = (None, None, None)


def _seg_block(tiles, name):
    b = SEG[name]
    t, r = divmod(b, COL_TILE // LANES)
    return tiles.index(t) * (COL_TILE // LANES) + r


def _params(dims, vmem_mb, flags=None):
    return pltpu.CompilerParams(dimension_semantics=dims, vmem_limit_bytes=vmem_mb * 1024 * 1024, flags=flags)


def _rms(x, g):
    ms = jnp.mean(x * x, axis=-1, keepdims=True)
    return x * lax.rsqrt(ms + NORM_EPS) * g


def _dot(a, b):
    return jnp.dot(a, b, preferred_element_type=F32)


def _dot_t(a, b):
    return lax.dot_general(a, b, (((1,), (1,)), ((), ())), preferred_element_type=F32)


def _dot_tn(a, b):
    return lax.dot_general(a, b, (((0,), (0,)), ((), ())), preferred_element_type=F32)


def _fold8(x, reduce_fn):
    return reduce_fn(x.reshape(x.shape[0] // SUBLANES, SUBLANES, x.shape[1]), axis=0)


def _staged_softmax(s_ref, p_ref, chunk, extra_logit=None):
    n_keys, n_q = s_ref.shape
    chunks = range(0, n_keys, chunk)
    m8 = functools.reduce(jnp.maximum, [_fold8(s_ref[c0:c0 + chunk, :], jnp.max) for c0 in chunks])
    m = jnp.max(m8, axis=0, keepdims=True)
    if extra_logit is not None:
        m = jnp.maximum(m, extra_logit)
    den8 = jnp.zeros((SUBLANES, n_q), F32)
    for c0 in chunks:
        e = jnp.exp2(s_ref[c0:c0 + chunk, :] - m)
        den8 = den8 + _fold8(e, jnp.sum)
        p_ref[c0:c0 + chunk, :] = e.astype(BF16)
    den = jnp.sum(den8, axis=0, keepdims=True)
    if extra_logit is not None:
        den = den + jnp.exp2(extra_logit - m)
    return den


def _colmax(parts):
    return functools.reduce(jnp.maximum, [jnp.max(p, axis=0, keepdims=True) for p in parts])


def _colsum(parts):
    return functools.reduce(jnp.add, [jnp.sum(p, axis=0, keepdims=True) for p in parts])


MOD_TN = 1024


def _mod_kernel(c_ref, w_ref, b_ref, o_ref):
    c = c_ref[...]
    cond = c * (1.0 / (1.0 + jnp.exp(-c)))
    o_ref[0] = _dot(cond.astype(BF16), w_ref[0].astype(BF16)) + b_ref[0]


def _mod_call(c_rows, w_mod, b_mod):
    n = N_MOD * D_MODEL
    return pl.pallas_call(
        _mod_kernel,
        grid=(DEPTH, n // MOD_TN),
        in_specs=[
            pl.BlockSpec((MOD_ROWS, D_MODEL), lambda l, j: (0, 0)),
            pl.BlockSpec((1, D_MODEL, MOD_TN), lambda l, j: (l, 0, j)),
            pl.BlockSpec((1, 1, MOD_TN), lambda l, j: (l, 0, j)),
        ],
        out_specs=pl.BlockSpec((1, MOD_ROWS, MOD_TN), lambda l, j: (l, 0, j)),
        out_shape=jax.ShapeDtypeStruct((DEPTH, MOD_ROWS, n), F32),
        compiler_params=_params(("arbitrary", "arbitrary"), 40),
        name="mod",
    )(c_rows, w_mod, b_mod.reshape(DEPTH, 1, n))


PROJ_TM = 512


def _proj_kernel(x_ref, shift_ref, scale_ref, g_ref, w_ref, rope_ref, hg_ref, o_ref, *, tiles):
    h = _rms(x_ref[...], g_ref[...]) * (1.0 + scale_ref[0]) + shift_ref[0]
    hb = h.astype(BF16)
    per = COL_TILE // LANES
    for n, t in enumerate(tiles):
        y = _dot(hb, w_ref[:, t * COL_TILE:(t + 1) * COL_TILE])
        for j in range(per):
            gain_row, rope, q_scale = BLOCK_KIND[t * per + j]
            yj = y[:, j * LANES:(j + 1) * LANES]
            if gain_row is not None:
                yj = _rms(yj, hg_ref[gain_row:gain_row + 1, :])
            if rope is not None:
                base = rope * 3 * LANES
                cos = rope_ref[:, base:base + LANES]
                sin_next = rope_ref[:, base + LANES:base + 2 * LANES]
                sin_prev = rope_ref[:, base + 2 * LANES:base + 3 * LANES]
                yj = (yj * cos + pltpu.roll(yj, LANES - 1, 1) * sin_next
                      + pltpu.roll(yj, 1, 1) * sin_prev)
            if q_scale is not None:
                yj = yj * q_scale
            o_ref[:, (n * per + j) * LANES:(n * per + j + 1) * LANES] = yj.astype(BF16)


def _proj_call(x2d, mod, g, w_in, layer, rope, head_gains, *, tiles, mod_row, rope_tile):
    t_tok = x2d.shape[0]
    tm = PROJ_TM
    wout = len(tiles) * COL_TILE
    return pl.pallas_call(
        functools.partial(_proj_kernel, tiles=tiles),
        grid=(t_tok // tm,),
        in_specs=[
            pl.BlockSpec((tm, D_MODEL), lambda i: (i, 0)),
            pl.BlockSpec((None, 1, 1, D_MODEL), lambda i: (layer, mod_row(i) * N_MOD + 0, 0, 0)),
            pl.BlockSpec((None, 1, 1, D_MODEL), lambda i: (layer, mod_row(i) * N_MOD + 1, 0, 0)),
            pl.BlockSpec((None, 1, D_MODEL), lambda i: (layer, 0, 0)),
            pl.BlockSpec((None, D_MODEL, PROJ_WIDTH), lambda i: (layer, 0, 0), pipeline_mode=pl.Buffered(1)),
            pl.BlockSpec((tm, 6 * LANES), lambda i: (rope_tile(i), 0)),
            pl.BlockSpec((None, 8, LANES), lambda i: (layer, 0, 0)),
        ],
        out_specs=pl.BlockSpec((tm, wout), lambda i: (i, 0)),
        out_shape=jax.ShapeDtypeStruct((t_tok, wout), BF16),
        compiler_params=_params(("arbitrary",), 56),
        name="proj",
    )(x2d, mod, mod, g, w_in, rope, head_gains)


A_TQ = 512
KEY_CHUNK = 256


def _attn_a_kernel(q_ref, k_ref, v_ref, kc_ref, vc_ref, o_ref, kall_ref, vall_ref, s_ref, p_ref):
    kall_ref[:SEQ, :] = k_ref[0]
    kall_ref[SEQ:, :] = kc_ref[0]
    vall_ref[:SEQ, :] = v_ref[0]
    vall_ref[SEQ:, :] = vc_ref[0]
    units = [(r0, g) for r0 in range(0, SEQ, A_TQ) for g in range(2)]

    def scores(n):
        r0, g = units[n]
        s_ref[n % 2] = _dot_t(kall_ref[...], q_ref[0, r0:r0 + A_TQ, g * LANES:(g + 1) * LANES])

    scores(0)
    for n, (r0, g) in enumerate(units):
        if n + 1 < len(units):
            scores(n + 1)
        den = _staged_softmax(s_ref.at[n % 2], p_ref, KEY_CHUNK)
        o_t = _dot_tn(vall_ref[...], p_ref[...])
        o_ref[0, r0:r0 + A_TQ, g * LANES:(g + 1) * LANES] = (o_t / den).T.astype(BF16)


def _attn_a_call(px, pc, ctx_tiles):
    b = px.shape[0]
    ak, av = SEG["ak"], SEG["av"]
    akc, avc = _seg_block(ctx_tiles, "ak"), _seg_block(ctx_tiles, "av")
    return pl.pallas_call(
        _attn_a_kernel,
        grid=(b, 2),
        in_specs=[
            pl.BlockSpec((1, SEQ, 2 * LANES), lambda i, h: (i, 0, h)),
            pl.BlockSpec((1, SEQ, LANES), lambda i, h: (i, 0, ak + h)),
            pl.BlockSpec((1, SEQ, LANES), lambda i, h: (i, 0, av + h)),
            pl.BlockSpec((1, CTX_LEN, LANES), lambda i, h: (i, 0, akc + h)),
            pl.BlockSpec((1, CTX_LEN, LANES), lambda i, h: (i, 0, avc + h)),
        ],
        out_specs=pl.BlockSpec((1, SEQ, 2 * LANES), lambda i, h: (i, 0, h)),
        out_shape=jax.ShapeDtypeStruct((b, SEQ, GROUP_WIDTH), BF16),
        scratch_shapes=[pltpu.VMEM((SEQ + CTX_LEN, LANES), BF16), pltpu.VMEM((SEQ + CTX_LEN, LANES), BF16),
                        pltpu.VMEM((2, SEQ + CTX_LEN, A_TQ), F32), pltpu.VMEM((SEQ + CTX_LEN, A_TQ), BF16)],
        compiler_params=_params(("arbitrary",) * 2, 48),
        name="attn_a",
    )(px, px, px, pc, pc)


NB_ROWS = SEQ // GRID_W
NB_QROWS = 4
NB_KROWS = NB_QROWS + WIN_ROWS - 1
NB_BLOCKS = NB_ROWS // NB_QROWS
NB_TQ = NB_QROWS * GRID_W
NB_KEYS = NB_KROWS * GRID_W
NB_CHUNK = (NB_KEYS + CTX_LEN) // 3


def _nb_window_start(j):
    return min(max(j * NB_QROWS - WIN_ROWS // 2, 0), NB_ROWS - NB_KROWS)


def _nb_class(j):
    return 0 if j == 0 else (2 if j == NB_BLOCKS - 1 else 1)


def _na_bias_table(rpb):
    n_dr, n_dc = rpb.shape[-2:]
    rpb = rpb.astype(F32).reshape(-1, n_dr, n_dc)
    n_tab = rpb.shape[0]
    period = 2 * GRID_W
    fill = jnp.full((n_tab, n_dr, period - n_dc), NEG_INF, F32)
    ext = jnp.concatenate([rpb[..., WIN_COLS - 1:], fill, rpb[..., :WIN_COLS - 1]], axis=-1)
    flat = jnp.tile(ext, (1, 1, GRID_W))[..., :GRID_W * (period - 1)]
    toep = flat.reshape(n_tab, n_dr, GRID_W, period - 1)[..., :GRID_W]
    c = jnp.arange(GRID_W)
    c_start = jnp.clip(c - WIN_COLS // 2, 0, GRID_W - WIN_COLS)
    kc = jnp.arange(GRID_W)
    in_win = (kc[None, :] >= c_start[:, None]) & (kc[None, :] < c_start[:, None] + WIN_COLS)
    toep = jnp.where(in_win, toep * LOG2E, NEG_INF)

    def masked_rows(n):
        return jnp.full((n_tab, n, GRID_W, GRID_W), NEG_INF, F32)

    classes = []
    for j in (0, 1, NB_BLOCKS - 1):
        q_rows = []
        for rq in range(NB_QROWS):
            r = j * NB_QROWS + rq
            r_start = min(max(r - WIN_ROWS // 2, 0), NB_ROWS - WIN_ROWS)
            lo = r_start - _nb_window_start(j)
            drow0 = r_start - r + WIN_ROWS - 1
            pieces = [masked_rows(lo), toep[:, drow0:drow0 + WIN_ROWS], masked_rows(NB_KROWS - WIN_ROWS - lo)]
            q_rows.append(jnp.concatenate([p for p in pieces if p.shape[1]], axis=1))
        classes.append(jnp.stack(q_rows, axis=1))
    table = jnp.stack(classes, axis=1).transpose(0, 1, 3, 5, 2, 4)
    return table.reshape(n_tab, 3, NB_KEYS, NB_TQ)


def _attn_b_kernel(q_ref, k_ref, v_ref, kc_ref, vc_ref, bias_ref, o_ref, s_ref, p_ref):
    def scores(j):
        q0, k0 = j * NB_TQ, _nb_window_start(j) * GRID_W
        q = q_ref[0, q0:q0 + NB_TQ, :]
        s_ref[j % 2, :NB_KEYS, :] = _dot_t(k_ref[0, k0:k0 + NB_KEYS, :], q) + bias_ref[0, _nb_class(j)]
        s_ref[j % 2, NB_KEYS:, :] = _dot_t(kc_ref[0], q)

    scores(0)
    for j in range(NB_BLOCKS):
        if j + 1 < NB_BLOCKS:
            scores(j + 1)
        q0, k0 = j * NB_TQ, _nb_window_start(j) * GRID_W
        den = _staged_softmax(s_ref.at[j % 2], p_ref, NB_CHUNK)
        o_t = (_dot_tn(v_ref[0, k0:k0 + NB_KEYS, :], p_ref[:NB_KEYS, :])
               + _dot_tn(vc_ref[0], p_ref[NB_KEYS:, :])) / den
        o_ref[0, q0:q0 + NB_TQ, :] = o_t.T.astype(BF16)


def _attn_b_call(px, pc, ctx_tiles, bias, layer):
    b = px.shape[0]
    bq, bk, bv = SEG["bq"], SEG["bk"], SEG["bv"]
    bkc, bvc = _seg_block(ctx_tiles, "bk"), _seg_block(ctx_tiles, "bv")
    return pl.pallas_call(
        _attn_b_kernel,
        grid=(b, 4),
        in_specs=[
            pl.BlockSpec((1, SEQ, LANES), lambda i, h: (i, 0, bq + h)),
            pl.BlockSpec((1, SEQ, LANES), lambda i, h: (i, 0, bk + h)),
            pl.BlockSpec((1, SEQ, LANES), lambda i, h: (i, 0, bv + h)),
            pl.BlockSpec((1, CTX_LEN, LANES), lambda i, h: (i, 0, bkc + h)),
            pl.BlockSpec((1, CTX_LEN, LANES), lambda i, h: (i, 0, bvc + h)),
            pl.BlockSpec((1, 3, NB_KEYS, NB_TQ), lambda i, h: (layer * 4 + h, 0, 0, 0)),
        ],
        out_specs=pl.BlockSpec((1, SEQ, LANES), lambda i, h: (i, 0, h)),
        out_shape=jax.ShapeDtypeStruct((b, SEQ, GROUP_WIDTH), BF16),
        scratch_shapes=[pltpu.VMEM((2, NB_KEYS + CTX_LEN, NB_TQ), F32),
                        pltpu.VMEM((NB_KEYS + CTX_LEN, NB_TQ), BF16)],
        compiler_params=_params(("arbitrary",) * 2, 32),
        name="attn_b",
    )(px, px, px, pc, pc, bias)


C_TQ = 512


def _diff_lambda(lam_ref, lam_init):
    a = jnp.sum(lam_ref[0:1, :] * lam_ref[1:2, :], axis=-1, keepdims=True)
    b = jnp.sum(lam_ref[2:3, :] * lam_ref[3:4, :], axis=-1, keepdims=True)
    return jnp.exp(a) - jnp.exp(b) + lam_init


def _diff_head(q, keys, vals, lam, gsub, lam_init):
    lane = lax.broadcasted_iota(jnp.int32, q.shape, 1)
    zero = jnp.zeros_like(q)
    maps = []
    for sub in range(2):
        qs = jnp.where((lane < DIFF_QK_DIM) == (sub == 0), q, zero)
        s = [_dot_t(k, qs) for k in keys]
        m = _colmax(s)
        e = [jnp.exp2(x - m) for x in s]
        maps.append((e, _colsum(e)))
    (e1, den1), (e2, den2) = maps
    ratio = lam * den1 / den2
    o_t = functools.reduce(jnp.add, [
        _dot_tn(v, (a - ratio * b).astype(BF16)) for a, b, v in zip(e1, e2, vals)]) / den1
    return _rms(o_t.T, gsub) * (1.0 - lam_init)


def _attn_c_kernel(lam_ref, gsub_ref, q_ref, k_ref, v_ref, kc_ref, vc_ref, o_ref,
                   kall_ref, vall_ref, s_ref, p_ref, *, lam_init):
    n_keys = SEQ + CTX_LEN
    kall_ref[:SEQ, :] = k_ref[0]
    kall_ref[SEQ:, :] = kc_ref[0]
    vall_ref[:SEQ, :] = v_ref[0]
    vall_ref[SEQ:, :] = vc_ref[0]
    lam = _diff_lambda(lam_ref, lam_init)
    lane = lax.broadcasted_iota(jnp.int32, (C_TQ, LANES), 1)
    tiles = list(range(0, SEQ, C_TQ))
    chunks = range(0, n_keys, KEY_CHUNK)

    def scores(n):
        q = q_ref[0, tiles[n]:tiles[n] + C_TQ, :]
        for sub in range(2):
            qs = jnp.where((lane < DIFF_QK_DIM) == (sub == 0), q, jnp.zeros_like(q))
            s_ref[n % 2, sub] = _dot_t(kall_ref[...], qs)

    scores(0)
    for n, r0 in enumerate(tiles):
        if n + 1 < len(tiles):
            scores(n + 1)
        slot = n % 2
        dens = []
        for sub in range(2):
            m8 = functools.reduce(jnp.maximum,
                                  [_fold8(s_ref[slot, sub, c0:c0 + KEY_CHUNK, :], jnp.max) for c0 in chunks])
            m = jnp.max(m8, axis=0, keepdims=True)
            den8 = jnp.zeros((SUBLANES, C_TQ), F32)
            for c0 in chunks:
                e = jnp.exp2(s_ref[slot, sub, c0:c0 + KEY_CHUNK, :] - m)
                den8 = den8 + _fold8(e, jnp.sum)
                s_ref[slot, sub, c0:c0 + KEY_CHUNK, :] = e
            dens.append(jnp.sum(den8, axis=0, keepdims=True))
        den1, den2 = dens
        ratio = lam * den1 / den2
        for c0 in chunks:
            p_ref[c0:c0 + KEY_CHUNK, :] = (s_ref[slot, 0, c0:c0 + KEY_CHUNK, :]
                                           - ratio * s_ref[slot, 1, c0:c0 + KEY_CHUNK, :]).astype(BF16)
        o_t = _dot_tn(vall_ref[...], p_ref[...]) / den1
        o = _rms(o_t.T, gsub_ref[...]) * (1.0 - lam_init)
        o_ref[0, r0:r0 + C_TQ, :] = o.astype(BF16)


def _attn_c_call(px, pc, ctx_tiles, lam_rows, gsub, layer, lam_init):
    b = px.shape[0]
    cq, ck, cv = SEG["cq"], SEG["ck"], SEG["cv"]
    ckc, cvc = _seg_block(ctx_tiles, "ck"), _seg_block(ctx_tiles, "cv")
    return pl.pallas_call(
        functools.partial(_attn_c_kernel, lam_init=lam_init),
        grid=(b, 4),
        in_specs=[
            pl.BlockSpec((None, 8, LANES), lambda i, h: (layer, 0, 0)),
            pl.BlockSpec((None, 1, LANES), lambda i, h: (layer, 0, 0)),
            pl.BlockSpec((1, SEQ, LANES), lambda i, h: (i, 0, cq + h)),
            pl.BlockSpec((1, SEQ, LANES), lambda i, h: (i, 0, ck + h)),
            pl.BlockSpec((1, SEQ, LANES), lambda i, h: (i, 0, cv + h)),
            pl.BlockSpec((1, CTX_LEN, LANES), lambda i, h: (i, 0, ckc + h)),
            pl.BlockSpec((1, CTX_LEN, LANES), lambda i, h: (i, 0, cvc + h)),
        ],
        out_specs=pl.BlockSpec((1, SEQ, LANES), lambda i, h: (i, 0, h)),
        out_shape=jax.ShapeDtypeStruct((b, SEQ, GROUP_WIDTH), BF16),
        scratch_shapes=[pltpu.VMEM((SEQ + CTX_LEN, LANES), BF16), pltpu.VMEM((SEQ + CTX_LEN, LANES), BF16),
                        pltpu.VMEM((2, 2, SEQ + CTX_LEN, C_TQ), F32), pltpu.VMEM((SEQ + CTX_LEN, C_TQ), BF16)],
        compiler_params=_params(("arbitrary",) * 2, 48),
        name="attn_c",
    )(lam_rows, gsub, px, px, px, pc, pc)


D_TQ = 256
D_SPAN = D_TQ + 2 * WINDOW


def _attn_d_kernel(sink_ref, q_ref, k_ref, v_ref, kc_ref, vc_ref, o_ref, s_ref, p_ref, *, layer):
    kvh = pl.program_id(1)
    col = lax.broadcasted_iota(jnp.int32, (D_SPAN, 2 * D_TQ), 1)
    rel = (col & (D_TQ - 1)) - lax.broadcasted_iota(jnp.int32, (D_SPAN, 2 * D_TQ), 0)
    head_col = lax.broadcasted_iota(jnp.int32, (1, 2 * D_TQ), 1)
    sink = jnp.where(head_col < D_TQ, sink_ref[layer, kvh * 2], sink_ref[layer, kvh * 2 + 1]) * LOG2E
    n_blocks = SEQ // D_TQ

    def key_start(i):
        return min(max(i * D_TQ - WINDOW, 0), SEQ - D_SPAN)

    def scores(i):
        q0, k0 = i * D_TQ, key_start(i)
        q = jnp.concatenate([q_ref[0, q0:q0 + D_TQ, :LANES], q_ref[0, q0:q0 + D_TQ, LANES:]], axis=0)
        valid = jnp.abs(rel + (q0 - k0)) <= WINDOW
        s_ref[i % 2, :D_SPAN, :] = jnp.where(valid, _dot_t(k_ref[0, k0:k0 + D_SPAN, :], q), NEG_INF)
        s_ref[i % 2, D_SPAN:, :] = _dot_t(kc_ref[0], q)

    scores(0)
    for i in range(n_blocks):
        if i + 1 < n_blocks:
            scores(i + 1)
        q0, k0 = i * D_TQ, key_start(i)
        den = _staged_softmax(s_ref.at[i % 2], p_ref, KEY_CHUNK, extra_logit=sink)
        o_t = (_dot_tn(v_ref[0, k0:k0 + D_SPAN, :], p_ref[:D_SPAN, :])
               + _dot_tn(vc_ref[0], p_ref[D_SPAN:, :])) / den
        o_ref[0, q0:q0 + D_TQ, :LANES] = o_t[:, :D_TQ].T.astype(BF16)
        o_ref[0, q0:q0 + D_TQ, LANES:] = o_t[:, D_TQ:].T.astype(BF16)


def _attn_d_call(px, pc, ctx_tiles, sink, layer):
    b = px.shape[0]
    dq, dk, dv = SEG["dq"], SEG["dk"], SEG["dv"]
    dkc, dvc = _seg_block(ctx_tiles, "dk"), _seg_block(ctx_tiles, "dv")
    return pl.pallas_call(
        functools.partial(_attn_d_kernel, layer=layer),
        grid=(b, 2),
        in_specs=[
            pl.BlockSpec(memory_space=pltpu.SMEM),
            pl.BlockSpec((1, SEQ, 2 * LANES), lambda i, h: (i, 0, dq // 2 + h)),
            pl.BlockSpec((1, SEQ, LANES), lambda i, h: (i, 0, dk + h)),
            pl.BlockSpec((1, SEQ, LANES), lambda i, h: (i, 0, dv + h)),
            pl.BlockSpec((1, CTX_LEN, LANES), lambda i, h: (i, 0, dkc + h)),
            pl.BlockSpec((1, CTX_LEN, LANES), lambda i, h: (i, 0, dvc + h)),
        ],
        out_specs=pl.BlockSpec((1, SEQ, 2 * LANES), lambda i, h: (i, 0, h)),
        out_shape=jax.ShapeDtypeStruct((b, SEQ, GROUP_WIDTH), BF16),
        scratch_shapes=[pltpu.VMEM((2, D_SPAN + CTX_LEN, 2 * D_TQ), F32),
                        pltpu.VMEM((D_SPAN + CTX_LEN, 2 * D_TQ), BF16)],
        compiler_params=_params(("arbitrary",) * 2, 32),
        name="attn_d",
    )(sink, px, px, px, pc, pc)


def _softmax_pv(q, k, v, sink=None):
    s = _dot_t(q, k)
    m = jnp.max(s, axis=-1, keepdims=True)
    if sink is not None:
        m = jnp.maximum(m, sink)
    e = jnp.exp2(s - m)
    den = jnp.sum(e, axis=-1, keepdims=True)
    if sink is not None:
        den = den + jnp.exp2(sink - m)
    return _dot(e.astype(BF16), v) / den


def _ctx_attn_kernel(sink_ref, lam_ref, gsub_ref, p_ref, o_ref, *, layer, lam_init):
    def blk(name, h):
        c = (SEG[name] + h) * LANES
        return p_ref[0, :, c:c + LANES]

    def put(group, h, o):
        c = group * GROUP_WIDTH + h * LANES
        o_ref[0, :, c:c + LANES] = o.astype(BF16)

    lam = _diff_lambda(lam_ref, lam_init)
    for h in range(4):
        put(0, h, _softmax_pv(blk("aq", h), blk("ak", h // 2), blk("av", h // 2)))
        put(1, h, _softmax_pv(blk("bq", h), blk("bk", h), blk("bv", h)))
        put(2, h, _diff_head(blk("cq", h), [blk("ck", h)], [blk("cv", h)], lam, gsub_ref[...], lam_init))
        put(3, h, _softmax_pv(blk("dq", h), blk("dk", h // 2), blk("dv", h // 2),
                              sink=sink_ref[layer, h] * LOG2E))


def _ctx_attn_call(pc, sink, lam_rows, gsub, layer, lam_init):
    b = pc.shape[0]
    return pl.pallas_call(
        functools.partial(_ctx_attn_kernel, layer=layer, lam_init=lam_init),
        grid=(b,),
        in_specs=[
            pl.BlockSpec(memory_space=pltpu.SMEM),
            pl.BlockSpec((None, 8, LANES), lambda i: (layer, 0, 0)),
            pl.BlockSpec((None, 1, LANES), lambda i: (layer, 0, 0)),
            pl.BlockSpec((1, CTX_LEN, PROJ_WIDTH), lambda i: (i, 0, 0)),
        ],
        out_specs=pl.BlockSpec((1, CTX_LEN, D_MODEL), lambda i: (i, 0, 0)),
        out_shape=jax.ShapeDtypeStruct((b, CTX_LEN, D_MODEL), BF16),
        compiler_params=_params(("arbitrary",), 32),
        name="ctx_attn",
    )(sink, lam_rows, gsub, pc)


POST_TM = 512
POST_TH = 1024


def _post_kernel(x_ref, oa_ref, ob_ref, oc_ref, od_ref, wout_ref, gate_mix_ref, shift_ref, scale_ref,
                 gate_mlp_ref, g_ref, wup_ref, wdn_ref, gfin_ref, out_ref, x1_ref, h_ref, *, final):
    j = pl.program_id(1)

    @pl.when(j == 0)
    def _():
        mix = functools.reduce(jnp.add, [
            _dot(o[...], wout_ref[n * GROUP_WIDTH:(n + 1) * GROUP_WIDTH, :])
            for n, o in enumerate((oa_ref, ob_ref, oc_ref, od_ref))])
        x1 = x_ref[...] + gate_mix_ref[0] * mix
        x1_ref[...] = x1
        h_ref[...] = (_rms(x1, g_ref[...]) * (1.0 + scale_ref[0]) + shift_ref[0]).astype(BF16)
        out_ref[...] = jnp.zeros_like(out_ref)

    u = _dot(h_ref[...], wup_ref[...])
    act = jnp.square(jnp.maximum(u, 0.0)).astype(BF16)
    out_ref[...] += _dot(act, wdn_ref[...])

    @pl.when(j == pl.num_programs(1) - 1)
    def _():
        y = x1_ref[...] + gate_mlp_ref[0] * out_ref[...]
        if final:
            y = _rms(y, gfin_ref[...])
        out_ref[...] = y


def _post_call(x2d, mix_parts, mix_blocks, w_out, mod, g_mlp, w_up, w_down, layer, g_final, *,
               mod_row, final):
    t_tok = x2d.shape[0]
    tm, th = POST_TM, POST_TH
    once = pl.Buffered(1)

    def mod_spec(k):
        return pl.BlockSpec((None, 1, 1, D_MODEL), lambda i, j: (layer, mod_row(i) * N_MOD + k, 0, 0))

    def part_spec(blk):
        return pl.BlockSpec((tm, GROUP_WIDTH), lambda i, j: (i, blk))

    return pl.pallas_call(
        functools.partial(_post_kernel, final=final),
        grid=(t_tok // tm, MLP_HIDDEN // th),
        in_specs=[
            pl.BlockSpec((tm, D_MODEL), lambda i, j: (i, 0)),
            *[part_spec(blk) for blk in mix_blocks],
            pl.BlockSpec((None, D_MODEL, D_MODEL), lambda i, j: (layer, 0, 0), pipeline_mode=once),
            mod_spec(2), mod_spec(3), mod_spec(4), mod_spec(5),
            pl.BlockSpec((None, 1, D_MODEL), lambda i, j: (layer, 0, 0)),
            pl.BlockSpec((None, D_MODEL, th), lambda i, j: (layer, 0, j)),
            pl.BlockSpec((None, th, D_MODEL), lambda i, j: (layer, j, 0)),
            pl.BlockSpec((1, D_MODEL), lambda i, j: (0, 0)),
        ],
        out_specs=pl.BlockSpec((tm, D_MODEL), lambda i, j: (i, 0)),
        out_shape=jax.ShapeDtypeStruct((t_tok, D_MODEL), F32),
        scratch_shapes=[pltpu.VMEM((tm, D_MODEL), F32), pltpu.VMEM((tm, D_MODEL), BF16)],
        compiler_params=_params(("arbitrary", "arbitrary"), 60),
        name="post",
    )(x2d, *mix_parts, w_out, mod, mod, mod, mod, g_mlp, w_up, w_down, g_final)


def _rope_rows(n_tok, dim):
    t = jnp.arange(n_tok)
    row = (t // GRID_W).astype(F32)
    col = (t % GRID_W).astype(F32)
    n_freq = dim // 4
    inv_freq = ROPE_THETA ** (-jnp.arange(n_freq, dtype=F32) / n_freq)
    ang = jnp.concatenate([row[:, None] * inv_freq, col[:, None] * inv_freq], axis=-1)
    cos = jnp.repeat(jnp.cos(ang), 2, axis=-1)
    sin = jnp.repeat(jnp.sin(ang), 2, axis=-1)
    even = (jnp.arange(dim) % 2) == 0
    parts = [cos, jnp.where(even, -sin, 0.0), jnp.where(even, 0.0, sin)]
    return [jnp.tile(p, (1, LANES // dim)) for p in parts]


def _rope_table(n_tok):
    return jnp.concatenate(_rope_rows(n_tok, HEAD_DIM) + _rope_rows(n_tok, DIFF_QK_DIM), axis=-1)


def _identity_rope_table(n_tok):
    one = jnp.ones((n_tok, LANES), F32)
    zero = jnp.zeros((n_tok, LANES), F32)
    return jnp.concatenate([one, zero, zero] * 2, axis=-1)


def _pad_rows(rows, n_rows=8):
    depth = rows[0].shape[0]
    padded = [jnp.pad(r.astype(F32), ((0, 0), (0, LANES - r.shape[1]))) for r in rows]
    padded += [jnp.zeros((depth, LANES), F32)] * (n_rows - len(rows))
    return jnp.stack(padded, axis=1)


def kernel(x, c, ctx, c_ctx, g_mix, g_mlp, w_mod, b_mod, w_in, w_out, gqa_gq, gqa_gk, na_rpb,
           diff_lq1, diff_lk1, diff_lq2, diff_lk2, diff_gsub, swa_sink, w_up, w_down, g_final):
    b, s, d = x.shape
    assert (s, d) == (SEQ, D_MODEL) and ctx.shape == (b, CTX_LEN, D_MODEL) and b <= CTX_MOD_ROW

    c_rows = jnp.concatenate(
        [c, jnp.zeros((CTX_MOD_ROW - b, D_MODEL), F32), c_ctx[None],
         jnp.zeros((MOD_ROWS - CTX_MOD_ROW - 1, D_MODEL), F32)], axis=0)
    mod = _mod_call(c_rows, w_mod, b_mod).reshape(DEPTH, MOD_ROWS * N_MOD, 1, D_MODEL)

    w_in_b = w_in.astype(BF16)
    w_out_b = w_out.astype(BF16)
    w_up_b = w_up.astype(BF16)
    w_down_b = w_down.astype(BF16)

    rope_x = _rope_table(SEQ)
    rope_c = _identity_rope_table(PROJ_TM)
    x_tiles_per_seq = SEQ // PROJ_TM
    x_post_tiles_per_seq = SEQ // POST_TM
    g_final2 = g_final.reshape(1, D_MODEL)
    g_mix3 = g_mix.reshape(DEPTH, 1, D_MODEL)
    g_mlp3 = g_mlp.reshape(DEPTH, 1, D_MODEL)
    head_gains = _pad_rows([gqa_gq, gqa_gk])
    lam_rows = _pad_rows([diff_lq1, diff_lk1, diff_lq2, diff_lk2])
    gsub = diff_gsub.reshape(DEPTH, 1, HEAD_DIM)
    sink = swa_sink.astype(F32)
    nb_bias = _na_bias_table(na_rpb)

    x2d = x.reshape(b * s, D_MODEL)
    c2d = ctx.reshape(b * CTX_LEN, D_MODEL)
    for l in range(DEPTH):
        with_ctx = l < DEPTH - 1
        lam_init = 0.8 - 0.6 * math.exp(-0.3 * l)
        ctx_tiles = ALL_TILES if with_ctx else KV_TILES

        px = _proj_call(x2d, mod, g_mix3, w_in_b, l, rope_x, head_gains, tiles=ALL_TILES,
                        mod_row=lambda i: i // x_tiles_per_seq, rope_tile=lambda i: i % x_tiles_per_seq)
        pc = _proj_call(c2d, mod, g_mix3, w_in_b, l, rope_c, head_gains, tiles=ctx_tiles,
                        mod_row=lambda i: CTX_MOD_ROW, rope_tile=lambda i: 0)
        px = px.reshape(b, SEQ, PROJ_WIDTH)
        pc = pc.reshape(b, CTX_LEN, len(ctx_tiles) * COL_TILE)

        oa = _attn_a_call(px, pc, ctx_tiles)
        ob = _attn_b_call(px, pc, ctx_tiles, nb_bias, l)
        oc = _attn_c_call(px, pc, ctx_tiles, lam_rows, gsub, l, lam_init)
        od = _attn_d_call(px, pc, ctx_tiles, sink, l)
        parts = [o.reshape(b * s, GROUP_WIDTH) for o in (oa, ob, oc, od)]
        if with_ctx:
            o_ctx = _ctx_attn_call(pc, sink, lam_rows, gsub, l, lam_init).reshape(b * CTX_LEN, D_MODEL)

        x2d = _post_call(x2d, parts, (0, 0, 0, 0), w_out_b, mod, g_mlp3, w_up_b, w_down_b, l, g_final2,
                         mod_row=lambda i: i // x_post_tiles_per_seq, final=not with_ctx)
        if with_ctx:
            c2d = _post_call(c2d, [o_ctx] * 4, (0, 1, 2, 3), w_out_b, mod, g_mlp3, w_up_b, w_down_b, l,
                             g_final2, mod_row=lambda i: CTX_MOD_ROW, final=False)
    return x2d.reshape(b, s, D_MODEL)
```

```python
import functools
import math

import jax
import jax.numpy as jnp
from jax import lax
from jax.experimental import pallas as pl
from jax.experimental.pallas import tpu as pltpu

D_MODEL = 2048
SEQ = 2048
DEPTH = 2
GRID_W = 64
CTX_LEN = 256
HEAD_DIM = 128
GROUP_WIDTH = 512
WIN_ROWS = 8
WIN_COLS = 16
DIFF_QK_DIM = 64
WINDOW = 128
MLP_HIDDEN = 4 * D_MODEL
N_MOD = 6
ROPE_THETA = 10000.0
NORM_EPS = 1e-6
NEG_INF = -1e30
PROJ_WIDTH = 5120
LOG2E = math.log2(math.e)

LANES = 128
SUBLANES = 8
COL_TILE = 512
MOD_ROWS = 16
CTX_MOD_ROW = 8

F32 = jnp.float32
BF16 = jnp.bfloat16

SEG = dict(aq=0, ak=4, av=6, bq=8, bk=12, bv=16, cq=20, ck=24, cv=28, dq=32, dk=36, dv=38)
ALL_TILES = tuple(range(PROJ_WIDTH // COL_TILE))
KV_TILES = (1, 3, 4, 6, 7, 9)
Q_SCALE_H = HEAD_DIM ** -0.5 * LOG2E
Q_SCALE_D = DIFF_QK_DIM ** -0.5 * LOG2E
_ROPE_H, _ROPE_D = 0, 1
BLOCK_KIND = {}
for _b in range(PROJ_WIDTH // LANES):
    if _b < 4:
        BLOCK_KIND[_b] = (0, _ROPE_H, Q_SCALE_H)
    elif _b < 6:
        BLOCK_KIND[_b] = (1, _ROPE_H, None)
    elif 8 <= _b < 12:
        BLOCK_KIND[_b] = (None, None, Q_SCALE_H)
    elif 20 <= _b < 24:
        BLOCK_KIND[_b] = (None, _ROPE_D, Q_SCALE_D)
    elif 24 <= _b < 28:
        BLOCK_KIND[_b] = (None, _ROPE_D, None)
    elif 32 <= _b < 36:
        BLOCK_KIND[_b] = (None, _ROPE_H, Q_SCALE_H)
    elif 36 <= _b < 38:
        BLOCK_KIND[_b] = (None, _ROPE_H, None)
    else:
        BLOCK_KIND[_b] = (None, None, None)


def _seg_block(tiles, name):
    b = SEG[name]
    t, r = divmod(b, COL_TILE // LANES)
    return tiles.index(t) * (COL_TILE // LANES) + r


def _params(dims, vmem_mb, flags=None):
    return pltpu.CompilerParams(dimension_semantics=dims, vmem_limit_bytes=vmem_mb * 1024 * 1024, flags=flags)


def _rms(x, g):
    ms = jnp.mean(x * x, axis=-1, keepdims=True)
    return x * lax.rsqrt(ms + NORM_EPS) * g


def _dot(a, b):
    return jnp.dot(a, b, preferred_element_type=F32)


def _dot_t(a, b):
    return lax.dot_general(a, b, (((1,), (1,)), ((), ())), preferred_element_type=F32)


def _dot_tn(a, b):
    return lax.dot_general(a, b, (((0,), (0,)), ((), ())), preferred_element_type=F32)


def _fold8(x, reduce_fn):
    return reduce_fn(x.reshape(x.shape[0] // SUBLANES, SUBLANES, x.shape[1]), axis=0)


def _staged_softmax(s_ref, p_ref, chunk, extra_logit=None):
    n_keys, n_q = s_ref.shape
    chunks = range(0, n_keys, chunk)
    m8 = functools.reduce(jnp.maximum, [_fold8(s_ref[c0:c0 + chunk, :], jnp.max) for c0 in chunks])
    m = jnp.max(m8, axis=0, keepdims=True)
    if extra_logit is not None:
        m = jnp.maximum(m, extra_logit)
    den8 = jnp.zeros((SUBLANES, n_q), F32)
    for c0 in chunks:
        e = jnp.exp2(s_ref[c0:c0 + chunk, :] - m)
        den8 = den8 + _fold8(e, jnp.sum)
        p_ref[c0:c0 + chunk, :] = e.astype(BF16)
    den = jnp.sum(den8, axis=0, keepdims=True)
    if extra_logit is not None:
        den = den + jnp.exp2(extra_logit - m)
    return den


def _colmax(parts):
    return functools.reduce(jnp.maximum, [jnp.max(p, axis=0, keepdims=True) for p in parts])


def _colsum(parts):
    return functools.reduce(jnp.add, [jnp.sum(p, axis=0, keepdims=True) for p in parts])


MOD_TN = 1024


def _mod_kernel(c_ref, w_ref, b_ref, o_ref):
    c = c_ref[...]
    cond = c * (1.0 / (1.0 + jnp.exp(-c)))
    o_ref[0] = _dot(cond.astype(BF16), w_ref[0].astype(BF16)) + b_ref[0]


def _mod_call(c_rows, w_mod, b_mod):
    n = N_MOD * D_MODEL
    return pl.pallas_call(
        _mod_kernel,
        grid=(DEPTH, n // MOD_TN),
        in_specs=[
            pl.BlockSpec((MOD_ROWS, D_MODEL), lambda l, j: (0, 0)),
            pl.BlockSpec((1, D_MODEL, MOD_TN), lambda l, j: (l, 0, j)),
            pl.BlockSpec((1, 1, MOD_TN), lambda l, j: (l, 0, j)),
        ],
        out_specs=pl.BlockSpec((1, MOD_ROWS, MOD_TN), lambda l, j: (l, 0, j)),
        out_shape=jax.ShapeDtypeStruct((DEPTH, MOD_ROWS, n), F32),
        compiler_params=_params(("arbitrary", "arbitrary"), 40),
        name="mod",
    )(c_rows, w_mod, b_mod.reshape(DEPTH, 1, n))


PROJ_TM = 512


def _proj_kernel(x_ref, shift_ref, scale_ref, g_ref, w_ref, rope_ref, hg_ref, *rest, tiles, n_cast):
    cast_src, o_ref, cast_dst = rest[:n_cast], rest[n_cast], rest[n_cast + 1:]
    for src, dst in zip(cast_src, cast_dst):
        dst[...] = src[...].astype(BF16)
    h = _rms(x_ref[...], g_ref[...]) * (1.0 + scale_ref[0]) + shift_ref[0]
    hb = h.astype(BF16)
    per = COL_TILE // LANES
    for n, t in enumerate(tiles):
        y = _dot(hb, w_ref[:, t * COL_TILE:(t + 1) * COL_TILE])
        for j in range(per):
            gain_row, rope, q_scale = BLOCK_KIND[t * per + j]
            yj = y[:, j * LANES:(j + 1) * LANES]
            if gain_row is not None:
                yj = _rms(yj, hg_ref[gain_row:gain_row + 1, :])
            if rope is not None:
                base = rope * 3 * LANES
                cos = rope_ref[:, base:base + LANES]
                sin_next = rope_ref[:, base + LANES:base + 2 * LANES]
                sin_prev = rope_ref[:, base + 2 * LANES:base + 3 * LANES]
                yj = (yj * cos + pltpu.roll(yj, LANES - 1, 1) * sin_next
                      + pltpu.roll(yj, 1, 1) * sin_prev)
            if q_scale is not None:
                yj = yj * q_scale
            o_ref[:, (n * per + j) * LANES:(n * per + j + 1) * LANES] = yj.astype(BF16)


def _proj_call(x2d, mod, g, w_in, layer, rope, head_gains, *, tiles, mod_row, rope_tile, cast=()):
    t_tok = x2d.shape[0]
    tm = PROJ_TM
    wout = len(tiles) * COL_TILE
    n_steps = t_tok // tm
    cast_in_specs, cast_out_specs, cast_shapes = [], [], []
    for w in cast:
        rows, cols = w.shape[1] // n_steps, w.shape[2]
        assert rows * n_steps == w.shape[1] and rows % 16 == 0
        cast_in_specs.append(pl.BlockSpec((None, rows, cols), lambda i: (layer, i, 0)))
        cast_out_specs.append(pl.BlockSpec((rows, cols), lambda i: (i, 0)))
        cast_shapes.append(jax.ShapeDtypeStruct(w.shape[1:], BF16))
    outs = pl.pallas_call(
        functools.partial(_proj_kernel, tiles=tiles, n_cast=len(cast)),
        grid=(n_steps,),
        in_specs=[
            pl.BlockSpec((tm, D_MODEL), lambda i: (i, 0)),
            pl.BlockSpec((None, 1, 1, D_MODEL), lambda i: (layer, mod_row(i) * N_MOD + 0, 0, 0)),
            pl.BlockSpec((None, 1, 1, D_MODEL), lambda i: (layer, mod_row(i) * N_MOD + 1, 0, 0)),
            pl.BlockSpec((None, 1, D_MODEL), lambda i: (layer, 0, 0)),
            pl.BlockSpec((None, D_MODEL, PROJ_WIDTH), lambda i: (layer, 0, 0), pipeline_mode=pl.Buffered(1)),
            pl.BlockSpec((tm, 6 * LANES), lambda i: (rope_tile(i), 0)),
            pl.BlockSpec((None, 8, LANES), lambda i: (layer, 0, 0)),
            *cast_in_specs,
        ],
        out_specs=[pl.BlockSpec((tm, wout), lambda i: (i, 0)), *cast_out_specs],
        out_shape=[jax.ShapeDtypeStruct((t_tok, wout), BF16), *cast_shapes],
        compiler_params=_params(("arbitrary",), 60),
        name="proj",
    )(x2d, mod, mod, g, w_in, rope, head_gains, *cast)
    return outs[0], outs[1:]


A_TQ = 512
KEY_CHUNK = 256


def _attn_a_kernel(q_ref, k_ref, v_ref, kc_ref, vc_ref, o_ref, kall_ref, vall_ref, s_ref, p_ref):
    kall_ref[:SEQ, :] = k_ref[0]
    kall_ref[SEQ:, :] = kc_ref[0]
    vall_ref[:SEQ, :] = v_ref[0]
    vall_ref[SEQ:, :] = vc_ref[0]
    units = [(r0, g) for r0 in range(0, SEQ, A_TQ) for g in range(2)]

    def scores(n):
        r0, g = units[n]
        s_ref[n % 2] = _dot_t(kall_ref[...], q_ref[0, r0:r0 + A_TQ, g * LANES:(g + 1) * LANES])

    scores(0)
    for n, (r0, g) in enumerate(units):
        if n + 1 < len(units):
            scores(n + 1)
        den = _staged_softmax(s_ref.at[n % 2], p_ref, KEY_CHUNK)
        o_t = _dot_tn(vall_ref[...], p_ref[...])
        o_ref[0, r0:r0 + A_TQ, g * LANES:(g + 1) * LANES] = (o_t / den).T.astype(BF16)


def _attn_a_call(px, pc, ctx_tiles):
    b = px.shape[0]
    ak, av = SEG["ak"], SEG["av"]
    akc, avc = _seg_block(ctx_tiles, "ak"), _seg_block(ctx_tiles, "av")
    return pl.pallas_call(
        _attn_a_kernel,
        grid=(b, 2),
        in_specs=[
            pl.BlockSpec((1, SEQ, 2 * LANES), lambda i, h: (i, 0, h)),
            pl.BlockSpec((1, SEQ, LANES), lambda i, h: (i, 0, ak + h)),
            pl.BlockSpec((1, SEQ, LANES), lambda i, h: (i, 0, av + h)),
            pl.BlockSpec((1, CTX_LEN, LANES), lambda i, h: (i, 0, akc + h)),
            pl.BlockSpec((1, CTX_LEN, LANES), lambda i, h: (i, 0, avc + h)),
        ],
        out_specs=pl.BlockSpec((1, SEQ, 2 * LANES), lambda i, h: (i, 0, h)),
        out_shape=jax.ShapeDtypeStruct((b, SEQ, GROUP_WIDTH), BF16),
        scratch_shapes=[pltpu.VMEM((SEQ + CTX_LEN, LANES), BF16), pltpu.VMEM((SEQ + CTX_LEN, LANES), BF16),
                        pltpu.VMEM((2, SEQ + CTX_LEN, A_TQ), F32), pltpu.VMEM((SEQ + CTX_LEN, A_TQ), BF16)],
        compiler_params=_params(("arbitrary",) * 2, 48),
        name="attn_a",
    )(px, px, px, pc, pc)


NB_ROWS = SEQ // GRID_W
NB_QROWS = 4
NB_KROWS = NB_QROWS + WIN_ROWS - 1
NB_BLOCKS = NB_ROWS // NB_QROWS
NB_TQ = NB_QROWS * GRID_W
NB_KEYS = NB_KROWS * GRID_W
NB_CHUNK = (NB_KEYS + CTX_LEN) // 3


def _nb_window_start(j):
    return min(max(j * NB_QROWS - WIN_ROWS // 2, 0), NB_ROWS - NB_KROWS)


def _nb_class(j):
    return 0 if j == 0 else (2 if j == NB_BLOCKS - 1 else 1)


def _na_bias_table(rpb):
    n_dr, n_dc = rpb.shape[-2:]
    rpb = rpb.astype(F32).reshape(-1, n_dr, n_dc)
    n_tab = rpb.shape[0]
    period = 2 * GRID_W
    fill = jnp.full((n_tab, n_dr, period - n_dc), NEG_INF, F32)
    ext = jnp.concatenate([rpb[..., WIN_COLS - 1::-1], fill, rpb[..., :WIN_COLS - 1:-1]], axis=-1)
    flat = jnp.tile(ext, (1, 1, GRID_W))[..., :GRID_W * (period - 1)]
    toep = flat.reshape(n_tab, n_dr, GRID_W, period - 1)[..., :GRID_W]
    c = jnp.arange(GRID_W)
    c_start = jnp.clip(c - WIN_COLS // 2, 0, GRID_W - WIN_COLS)
    kc = jnp.arange(GRID_W)
    in_win = (kc[:, None] >= c_start[None, :]) & (kc[:, None] < c_start[None, :] + WIN_COLS)
    toep = jnp.where(in_win, toep * LOG2E, NEG_INF)

    def masked_rows(n):
        return jnp.full((n_tab, n, GRID_W, GRID_W), NEG_INF, F32)

    classes = []
    for j in (0, 1, NB_BLOCKS - 1):
        q_rows = []
        for rq in range(NB_QROWS):
            r = j * NB_QROWS + rq
            r_start = min(max(r - WIN_ROWS // 2, 0), NB_ROWS - WIN_ROWS)
            lo = r_start - _nb_window_start(j)
            drow0 = r_start - r + WIN_ROWS - 1
            pieces = [masked_rows(lo), toep[:, drow0:drow0 + WIN_ROWS], masked_rows(NB_KROWS - WIN_ROWS - lo)]
            q_rows.append(jnp.concatenate([p for p in pieces if p.shape[1]], axis=1))
        classes.append(jnp.stack(q_rows, axis=3))
    return jnp.stack(classes, axis=1).reshape(n_tab, 3, NB_KEYS, NB_TQ)


def _attn_b_kernel(q_ref, k_ref, v_ref, kc_ref, vc_ref, bias_ref, o_ref, s_ref, p_ref):
    def scores(j):
        q0, k0 = j * NB_TQ, _nb_window_start(j) * GRID_W
        q = q_ref[0, q0:q0 + NB_TQ, :]
        s_ref[j % 2, :NB_KEYS, :] = _dot_t(k_ref[0, k0:k0 + NB_KEYS, :], q) + bias_ref[0, _nb_class(j)]
        s_ref[j % 2, NB_KEYS:, :] = _dot_t(kc_ref[0], q)

    scores(0)
    for j in range(NB_BLOCKS):
        if j + 1 < NB_BLOCKS:
            scores(j + 1)
        q0, k0 = j * NB_TQ, _nb_window_start(j) * GRID_W
        den = _staged_softmax(s_ref.at[j % 2], p_ref, NB_CHUNK)
        o_t = (_dot_tn(v_ref[0, k0:k0 + NB_KEYS, :], p_ref[:NB_KEYS, :])
               + _dot_tn(vc_ref[0], p_ref[NB_KEYS:, :])) / den
        o_ref[0, q0:q0 + NB_TQ, :] = o_t.T.astype(BF16)


def _attn_b_call(px, pc, ctx_tiles, bias, layer):
    b = px.shape[0]
    bq, bk, bv = SEG["bq"], SEG["bk"], SEG["bv"]
    bkc, bvc = _seg_block(ctx_tiles, "bk"), _seg_block(ctx_tiles, "bv")
    return pl.pallas_call(
        _attn_b_kernel,
        grid=(b, 4),
        in_specs=[
            pl.BlockSpec((1, SEQ, LANES), lambda i, h: (i, 0, bq + h)),
            pl.BlockSpec((1, SEQ, LANES), lambda i, h: (i, 0, bk + h)),
            pl.BlockSpec((1, SEQ, LANES), lambda i, h: (i, 0, bv + h)),
            pl.BlockSpec((1, CTX_LEN, LANES), lambda i, h: (i, 0, bkc + h)),
            pl.BlockSpec((1, CTX_LEN, LANES), lambda i, h: (i, 0, bvc + h)),
            pl.BlockSpec((1, 3, NB_KEYS, NB_TQ), lambda i, h: (layer * 4 + h, 0, 0, 0)),
        ],
        out_specs=pl.BlockSpec((1, SEQ, LANES), lambda i, h: (i, 0, h)),
        out_shape=jax.ShapeDtypeStruct((b, SEQ, GROUP_WIDTH), BF16),
        scratch_shapes=[pltpu.VMEM((2, NB_KEYS + CTX_LEN, NB_TQ), F32),
                        pltpu.VMEM((NB_KEYS + CTX_LEN, NB_TQ), BF16)],
        compiler_params=_params(("arbitrary",) * 2, 32),
        name="attn_b",
    )(px, px, px, pc, pc, bias)


C_TQ = 512


def _diff_lambda(lam_ref, lam_init):
    a = jnp.sum(lam_ref[0:1, :] * lam_ref[1:2, :], axis=-1, keepdims=True)
    b = jnp.sum(lam_ref[2:3, :] * lam_ref[3:4, :], axis=-1, keepdims=True)
    return jnp.exp(a) - jnp.exp(b) + lam_init


def _diff_head(q, keys, vals, lam, gsub, lam_init):
    lane = lax.broadcasted_iota(jnp.int32, q.shape, 1)
    zero = jnp.zeros_like(q)
    maps = []
    for sub in range(2):
        qs = jnp.where((lane < DIFF_QK_DIM) == (sub == 0), q, zero)
        s = [_dot_t(k, qs) for k in keys]
        m = _colmax(s)
        e = [jnp.exp2(x - m) for x in s]
        maps.append((e, _colsum(e)))
    (e1, den1), (e2, den2) = maps
    ratio = lam * den1 / den2
    o_t = functools.reduce(jnp.add, [
        _dot_tn(v, (a - ratio * b).astype(BF16)) for a, b, v in zip(e1, e2, vals)]) / den1
    return _rms(o_t.T, gsub) * (1.0 - lam_init)


def _attn_c_kernel(lam_ref, gsub_ref, q_ref, k_ref, v_ref, kc_ref, vc_ref, o_ref,
                   kall_ref, vall_ref, s_ref, p_ref, *, lam_init):
    n_keys = SEQ + CTX_LEN
    kall_ref[:SEQ, :] = k_ref[0]
    kall_ref[SEQ:, :] = kc_ref[0]
    vall_ref[:SEQ, :] = v_ref[0]
    vall_ref[SEQ:, :] = vc_ref[0]
    lam = _diff_lambda(lam_ref, lam_init)
    lane = lax.broadcasted_iota(jnp.int32, (C_TQ, LANES), 1)
    tiles = list(range(0, SEQ, C_TQ))
    chunks = range(0, n_keys, KEY_CHUNK)

    def scores(n):
        q = q_ref[0, tiles[n]:tiles[n] + C_TQ, :]
        for sub in range(2):
            qs = jnp.where((lane < DIFF_QK_DIM) == (sub == 0), q, jnp.zeros_like(q))
            s_ref[n % 2, sub] = _dot_t(kall_ref[...], qs)

    scores(0)
    for n, r0 in enumerate(tiles):
        if n + 1 < len(tiles):
            scores(n + 1)
        slot = n % 2
        dens = []
        for sub in range(2):
            m8 = functools.reduce(jnp.maximum,
                                  [_fold8(s_ref[slot, sub, c0:c0 + KEY_CHUNK, :], jnp.max) for c0 in chunks])
            m = jnp.max(m8, axis=0, keepdims=True)
            den8 = jnp.zeros((SUBLANES, C_TQ), F32)
            for c0 in chunks:
                e = jnp.exp2(s_ref[slot, sub, c0:c0 + KEY_CHUNK, :] - m)
                den8 = den8 + _fold8(e, jnp.sum)
                s_ref[slot, sub, c0:c0 + KEY_CHUNK, :] = e
            dens.append(jnp.sum(den8, axis=0, keepdims=True))
        den1, den2 = dens
        ratio = lam * den1 / den2
        for c0 in chunks:
            p_ref[c0:c0 + KEY_CHUNK, :] = (s_ref[slot, 0, c0:c0 + KEY_CHUNK, :]
                                           - ratio * s_ref[slot, 1, c0:c0 + KEY_CHUNK, :]).astype(BF16)
        o_t = _dot_tn(vall_ref[...], p_ref[...]) / den1
        o = _rms(o_t.T, gsub_ref[...]) * (1.0 - lam_init)
        o_ref[0, r0:r0 + C_TQ, :] = o.astype(BF16)


def _attn_c_call(px, pc, ctx_tiles, lam_rows, gsub, layer, lam_init):
    b = px.shape[0]
    cq, ck, cv = SEG["cq"], SEG["ck"], SEG["cv"]
    ckc, cvc = _seg_block(ctx_tiles, "ck"), _seg_block(ctx_tiles, "cv")
    return pl.pallas_call(
        functools.partial(_attn_c_kernel, lam_init=lam_init),
        grid=(b, 4),
        in_specs=[
            pl.BlockSpec((None, 8, LANES), lambda i, h: (layer, 0, 0)),
            pl.BlockSpec((None, 1, LANES), lambda i, h: (layer, 0, 0)),
            pl.BlockSpec((1, SEQ, LANES), lambda i, h: (i, 0, cq + h)),
            pl.BlockSpec((1, SEQ, LANES), lambda i, h: (i, 0, ck + h)),
            pl.BlockSpec((1, SEQ, LANES), lambda i, h: (i, 0, cv + h)),
            pl.BlockSpec((1, CTX_LEN, LANES), lambda i, h: (i, 0, ckc + h)),
            pl.BlockSpec((1, CTX_LEN, LANES), lambda i, h: (i, 0, cvc + h)),
        ],
        out_specs=pl.BlockSpec((1, SEQ, LANES), lambda i, h: (i, 0, h)),
        out_shape=jax.ShapeDtypeStruct((b, SEQ, GROUP_WIDTH), BF16),
        scratch_shapes=[pltpu.VMEM((SEQ + CTX_LEN, LANES), BF16), pltpu.VMEM((SEQ + CTX_LEN, LANES), BF16),
                        pltpu.VMEM((2, 2, SEQ + CTX_LEN, C_TQ), F32), pltpu.VMEM((SEQ + CTX_LEN, C_TQ), BF16)],
        compiler_params=_params(("arbitrary",) * 2, 48),
        name="attn_c",
    )(lam_rows, gsub, px, px, px, pc, pc)


D_TQ = 256
D_SPAN = D_TQ + 2 * WINDOW


def _attn_d_kernel(sink_ref, q_ref, k_ref, v_ref, kc_ref, vc_ref, o_ref, s_ref, p_ref, *, layer):
    kvh = pl.program_id(1)
    col = lax.broadcasted_iota(jnp.int32, (D_SPAN, 2 * D_TQ), 1)
    rel = (col & (D_TQ - 1)) - lax.broadcasted_iota(jnp.int32, (D_SPAN, 2 * D_TQ), 0)
    head_col = lax.broadcasted_iota(jnp.int32, (1, 2 * D_TQ), 1)
    sink = jnp.where(head_col < D_TQ, sink_ref[layer, kvh * 2], sink_ref[layer, kvh * 2 + 1]) * LOG2E
    n_blocks = SEQ // D_TQ

    def key_start(i):
        return min(max(i * D_TQ - WINDOW, 0), SEQ - D_SPAN)

    def scores(i):
        q0, k0 = i * D_TQ, key_start(i)
        q = jnp.concatenate([q_ref[0, q0:q0 + D_TQ, :LANES], q_ref[0, q0:q0 + D_TQ, LANES:]], axis=0)
        valid = jnp.abs(rel + (q0 - k0)) <= WINDOW
        s_ref[i % 2, :D_SPAN, :] = jnp.where(valid, _dot_t(k_ref[0, k0:k0 + D_SPAN, :], q), NEG_INF)
        s_ref[i % 2, D_SPAN:, :] = _dot_t(kc_ref[0], q)

    scores(0)
    for i in range(n_blocks):
        if i + 1 < n_blocks:
            scores(i + 1)
        q0, k0 = i * D_TQ, key_start(i)
        den = _staged_softmax(s_ref.at[i % 2], p_ref, KEY_CHUNK, extra_logit=sink)
        o_t = (_dot_tn(v_ref[0, k0:k0 + D_SPAN, :], p_ref[:D_SPAN, :])
               + _dot_tn(vc_ref[0], p_ref[D_SPAN:, :])) / den
        o_ref[0, q0:q0 + D_TQ, :LANES] = o_t[:, :D_TQ].T.astype(BF16)
        o_ref[0, q0:q0 + D_TQ, LANES:] = o_t[:, D_TQ:].T.astype(BF16)


def _attn_d_call(px, pc, ctx_tiles, sink, layer):
    b = px.shape[0]
    dq, dk, dv = SEG["dq"], SEG["dk"], SEG["dv"]
    dkc, dvc = _seg_block(ctx_tiles, "dk"), _seg_block(ctx_tiles, "dv")
    return pl.pallas_call(
        functools.partial(_attn_d_kernel, layer=layer),
        grid=(b, 2),
        in_specs=[
            pl.BlockSpec(memory_space=pltpu.SMEM),
            pl.BlockSpec((1, SEQ, 2 * LANES), lambda i, h: (i, 0, dq // 2 + h)),
            pl.BlockSpec((1, SEQ, LANES), lambda i, h: (i, 0, dk + h)),
            pl.BlockSpec((1, SEQ, LANES), lambda i, h: (i, 0, dv + h)),
            pl.BlockSpec((1, CTX_LEN, LANES), lambda i, h: (i, 0, dkc + h)),
            pl.BlockSpec((1, CTX_LEN, LANES), lambda i, h: (i, 0, dvc + h)),
        ],
        out_specs=pl.BlockSpec((1, SEQ, 2 * LANES), lambda i, h: (i, 0, h)),
        out_shape=jax.ShapeDtypeStruct((b, SEQ, GROUP_WIDTH), BF16),
        scratch_shapes=[pltpu.VMEM((2, D_SPAN + CTX_LEN, 2 * D_TQ), F32),
                        pltpu.VMEM((D_SPAN + CTX_LEN, 2 * D_TQ), BF16)],
        compiler_params=_params(("arbitrary",) * 2, 32),
        name="attn_d",
    )(sink, px, px, px, pc, pc)


def _softmax_pv(q, k, v, sink=None):
    s = _dot_t(q, k)
    m = jnp.max(s, axis=-1, keepdims=True)
    if sink is not None:
        m = jnp.maximum(m, sink)
    e = jnp.exp2(s - m)
    den = jnp.sum(e, axis=-1, keepdims=True)
    if sink is not None:
        den = den + jnp.exp2(sink - m)
    return _dot(e.astype(BF16), v) / den


def _ctx_attn_kernel(sink_ref, lam_ref, gsub_ref, p_ref, o_ref, *, layer, lam_init):
    def blk(name, h):
        c = (SEG[name] + h) * LANES
        return p_ref[0, :, c:c + LANES]

    def put(group, h, o):
        c = group * GROUP_WIDTH + h * LANES
        o_ref[0, :, c:c + LANES] = o.astype(BF16)

    lam = _diff_lambda(lam_ref, lam_init)
    for h in range(4):
        put(0, h, _softmax_pv(blk("aq", h), blk("ak", h // 2), blk("av", h // 2)))
        put(1, h, _softmax_pv(blk("bq", h), blk("bk", h), blk("bv", h)))
        put(2, h, _diff_head(blk("cq", h), [blk("ck", h)], [blk("cv", h)], lam, gsub_ref[...], lam_init))
        put(3, h, _softmax_pv(blk("dq", h), blk("dk", h // 2), blk("dv", h // 2),
                              sink=sink_ref[layer, h] * LOG2E))


def _ctx_attn_call(pc, sink, lam_rows, gsub, layer, lam_init):
    b = pc.shape[0]
    return pl.pallas_call(
        functools.partial(_ctx_attn_kernel, layer=layer, lam_init=lam_init),
        grid=(b,),
        in_specs=[
            pl.BlockSpec(memory_space=pltpu.SMEM),
            pl.BlockSpec((None, 8, LANES), lambda i: (layer, 0, 0)),
            pl.BlockSpec((None, 1, LANES), lambda i: (layer, 0, 0)),
            pl.BlockSpec((1, CTX_LEN, PROJ_WIDTH), lambda i: (i, 0, 0)),
        ],
        out_specs=pl.BlockSpec((1, CTX_LEN, D_MODEL), lambda i: (i, 0, 0)),
        out_shape=jax.ShapeDtypeStruct((b, CTX_LEN, D_MODEL), BF16),
        compiler_params=_params(("arbitrary",), 32),
        name="ctx_attn",
    )(sink, lam_rows, gsub, pc)


POST_TM = 512
POST_TH = 1024
POST_ROWS = 128


def _post_kernel(x_ref, oa_ref, ob_ref, oc_ref, od_ref, wout_ref, gate_mix_ref, shift_ref, scale_ref,
                 gate_mlp_ref, g_ref, wup_ref, wdn_ref, gfin_ref, out_ref, x1_ref, h_ref, *, final):
    j = pl.program_id(1)

    @pl.when(j == 0)
    def _():
        for r0 in range(0, x_ref.shape[0], POST_ROWS):
            rows = slice(r0, r0 + POST_ROWS)
            mix = functools.reduce(jnp.add, [
                _dot(o[rows, :], wout_ref[n * GROUP_WIDTH:(n + 1) * GROUP_WIDTH, :])
                for n, o in enumerate((oa_ref, ob_ref, oc_ref, od_ref))])
            x1 = x_ref[rows, :] + gate_mix_ref[0] * mix
            x1_ref[rows, :] = x1
            h_ref[rows, :] = (_rms(x1, g_ref[...]) * (1.0 + scale_ref[0]) + shift_ref[0]).astype(BF16)
        out_ref[...] = jnp.zeros_like(out_ref)

    u = _dot(h_ref[...], wup_ref[...])
    act = jnp.square(jnp.maximum(u, 0.0)).astype(BF16)
    out_ref[...] += _dot(act, wdn_ref[...])

    @pl.when(j == pl.num_programs(1) - 1)
    def _():
        y = x1_ref[...] + gate_mlp_ref[0] * out_ref[...]
        if final:
            y = _rms(y, gfin_ref[...])
        out_ref[...] = y


def _post_call(x2d, mix_parts, mix_blocks, w_out, mod, g_mlp, w_up, w_down, layer, g_final, *,
               mod_row, final):
    t_tok = x2d.shape[0]
    tm, th = POST_TM, POST_TH
    once = pl.Buffered(1)

    def mod_spec(k):
        return pl.BlockSpec((None, 1, 1, D_MODEL), lambda i, j: (layer, mod_row(i) * N_MOD + k, 0, 0))

    def part_spec(blk):
        return pl.BlockSpec((tm, GROUP_WIDTH), lambda i, j: (i, blk))

    return pl.pallas_call(
        functools.partial(_post_kernel, final=final),
        grid=(t_tok // tm, MLP_HIDDEN // th),
        in_specs=[
            pl.BlockSpec((tm, D_MODEL), lambda i, j: (i, 0)),
            *[part_spec(blk) for blk in mix_blocks],
            pl.BlockSpec((D_MODEL, D_MODEL), lambda i, j: (0, 0), pipeline_mode=once),
            mod_spec(2), mod_spec(3), mod_spec(4), mod_spec(5),
            pl.BlockSpec((None, 1, D_MODEL), lambda i, j: (layer, 0, 0)),
            pl.BlockSpec((D_MODEL, th), lambda i, j: (0, j)),
            pl.BlockSpec((th, D_MODEL), lambda i, j: (j, 0)),
            pl.BlockSpec((1, D_MODEL), lambda i, j: (0, 0)),
        ],
        out_specs=pl.BlockSpec((tm, D_MODEL), lambda i, j: (i, 0)),
        out_shape=jax.ShapeDtypeStruct((t_tok, D_MODEL), F32),
        scratch_shapes=[pltpu.VMEM((tm, D_MODEL), F32), pltpu.VMEM((tm, D_MODEL), BF16)],
        compiler_params=_params(("arbitrary", "arbitrary"), 60),
        name="post",
    )(x2d, *mix_parts, w_out, mod, mod, mod, mod, g_mlp, w_up, w_down, g_final)


def _rope_rows(n_tok, dim):
    t = jnp.arange(n_tok)
    row = (t // GRID_W).astype(F32)
    col = (t % GRID_W).astype(F32)
    n_freq = dim // 4
    inv_freq = ROPE_THETA ** (-jnp.arange(n_freq, dtype=F32) / n_freq)
    ang = jnp.concatenate([row[:, None] * inv_freq, col[:, None] * inv_freq], axis=-1)
    cos = jnp.repeat(jnp.cos(ang), 2, axis=-1)
    sin = jnp.repeat(jnp.sin(ang), 2, axis=-1)
    even = (jnp.arange(dim) % 2) == 0
    parts = [cos, jnp.where(even, -sin, 0.0), jnp.where(even, 0.0, sin)]
    return [jnp.tile(p, (1, LANES // dim)) for p in parts]


def _rope_table(n_tok):
    return jnp.concatenate(_rope_rows(n_tok, HEAD_DIM) + _rope_rows(n_tok, DIFF_QK_DIM), axis=-1)


def _identity_rope_table(n_tok):
    one = jnp.ones((n_tok, LANES), F32)
    zero = jnp.zeros((n_tok, LANES), F32)
    return jnp.concatenate([one, zero, zero] * 2, axis=-1)


def _pad_rows(rows, n_rows=8):
    depth = rows[0].shape[0]
    padded = [jnp.pad(r.astype(F32), ((0, 0), (0, LANES - r.shape[1]))) for r in rows]
    padded += [jnp.zeros((depth, LANES), F32)] * (n_rows - len(rows))
    return jnp.stack(padded, axis=1)


def kernel(x, c, ctx, c_ctx, g_mix, g_mlp, w_mod, b_mod, w_in, w_out, gqa_gq, gqa_gk, na_rpb,
           diff_lq1, diff_lk1, diff_lq2, diff_lk2, diff_gsub, swa_sink, w_up, w_down, g_final):
    b, s, d = x.shape
    assert (s, d) == (SEQ, D_MODEL) and ctx.shape == (b, CTX_LEN, D_MODEL) and b <= CTX_MOD_ROW

    c_rows = jnp.concatenate(
        [c, jnp.zeros((CTX_MOD_ROW - b, D_MODEL), F32), c_ctx[None],
         jnp.zeros((MOD_ROWS - CTX_MOD_ROW - 1, D_MODEL), F32)], axis=0)
    mod = _mod_call(c_rows, w_mod, b_mod).reshape(DEPTH, MOD_ROWS * N_MOD, 1, D_MODEL)

    w_in_b = w_in.astype(BF16)

    rope_x = _rope_table(SEQ)
    rope_c = _identity_rope_table(PROJ_TM)
    x_tiles_per_seq = SEQ // PROJ_TM
    x_post_tiles_per_seq = SEQ // POST_TM
    g_final2 = g_final.reshape(1, D_MODEL)
    g_mix3 = g_mix.reshape(DEPTH, 1, D_MODEL)
    g_mlp3 = g_mlp.reshape(DEPTH, 1, D_MODEL)
    head_gains = _pad_rows([gqa_gq, gqa_gk])
    lam_rows = _pad_rows([diff_lq1, diff_lk1, diff_lq2, diff_lk2])
    gsub = diff_gsub.reshape(DEPTH, 1, HEAD_DIM)
    sink = swa_sink.astype(F32)
    nb_bias = _na_bias_table(na_rpb)

    x2d = x.reshape(b * s, D_MODEL)
    c2d = ctx.reshape(b * CTX_LEN, D_MODEL)
    for l in range(DEPTH):
        with_ctx = l < DEPTH - 1
        lam_init = 0.8 - 0.6 * math.exp(-0.3 * l)
        ctx_tiles = ALL_TILES if with_ctx else KV_TILES

        px, (w_out_b, w_up_b, w_down_b) = _proj_call(
            x2d, mod, g_mix3, w_in_b, l, rope_x, head_gains, tiles=ALL_TILES, cast=(w_out, w_up, w_down),
            mod_row=lambda i: i // x_tiles_per_seq, rope_tile=lambda i: i % x_tiles_per_seq)
        pc, _ = _proj_call(c2d, mod, g_mix3, w_in_b, l, rope_c, head_gains, tiles=ctx_tiles,
                           mod_row=lambda i: CTX_MOD_ROW, rope_tile=lambda i: 0)
        px = px.reshape(b, SEQ, PROJ_WIDTH)
        pc = pc.reshape(b, CTX_LEN, len(ctx_tiles) * COL_TILE)

        oa = _attn_a_call(px, pc, ctx_tiles)
        ob = _attn_b_call(px, pc, ctx_tiles, nb_bias, l)
        oc = _attn_c_call(px, pc, ctx_tiles, lam_rows, gsub, l, lam_init)
        od = _attn_d_call(px, pc, ctx_tiles, sink, l)
        parts = [o.reshape(b * s, GROUP_WIDTH) for o in (oa, ob, oc, od)]
        if with_ctx:
            o_ctx = _ctx_attn_call(pc, sink, lam_rows, gsub, l, lam_init).reshape(b * CTX_LEN, D_MODEL)

        x2d = _post_call(x2d, parts, (0, 0, 0, 0), w_out_b, mod, g_mlp3, w_up_b, w_down_b, l, g_final2,
                         mod_row=lambda i: i // x_post_tiles_per_seq, final=not with_ctx)
        if with_ctx:
            c2d = _post_call(c2d, [o_ctx] * 4, (0, 1, 2, 3), w_out_b, mod, g_mlp3, w_up_b, w_down_b, l,
                             g_final2, mod_row=lambda i: CTX_MOD_ROW, final=False)
    return x2d.reshape(b, s, D_MODEL)
```

```python
import functools
import math

import jax
import jax.numpy as jnp
from jax import lax
from jax.experimental import pallas as pl
from jax.experimental.pallas import tpu as pltpu

D_MODEL = 2048
SEQ = 2048
DEPTH = 2
GRID_W = 64
CTX_LEN = 256
HEAD_DIM = 128
GROUP_WIDTH = 512
WIN_ROWS = 8
WIN_COLS = 16
DIFF_QK_DIM = 64
WINDOW = 128
MLP_HIDDEN = 4 * D_MODEL
N_MOD = 6
ROPE_THETA = 10000.0
NORM_EPS = 1e-6
NEG_INF = -1e30
PROJ_WIDTH = 5120
LOG2E = math.log2(math.e)

LANES = 128
SUBLANES = 8
COL_TILE = 512
MOD_ROWS = 16
CTX_MOD_ROW = 8

F32 = jnp.float32
BF16 = jnp.bfloat16

SEG = dict(aq=0, ak=4, av=6, bq=8, bk=12, bv=16, cq=20, ck=24, cv=28, dq=32, dk=36, dv=38)
ALL_TILES = tuple(range(PROJ_WIDTH // COL_TILE))
KV_TILES = (1, 3, 4, 6, 7, 9)
Q_SCALE_H = HEAD_DIM ** -0.5 * LOG2E
Q_SCALE_D = DIFF_QK_DIM ** -0.5 * LOG2E
_ROPE_H, _ROPE_D = 0, 1
BLOCK_KIND = {}
for _b in range(PROJ_WIDTH // LANES):
    if _b < 4:
        BLOCK_KIND[_b] = (0, _ROPE_H, Q_SCALE_H)
    elif _b < 6:
        BLOCK_KIND[_b] = (1, _ROPE_H, None)
    elif 8 <= _b < 12:
        BLOCK_KIND[_b] = (None, None, Q_SCALE_H)
    elif 20 <= _b < 24:
        BLOCK_KIND[_b] = (None, _ROPE_D, Q_SCALE_D)
    elif 24 <= _b < 28:
        BLOCK_KIND[_b] = (None, _ROPE_D, None)
    elif 32 <= _b < 36:
        BLOCK_KIND[_b] = (None, _ROPE_H, Q_SCALE_H)
    elif 36 <= _b < 38:
        BLOCK_KIND[_b] = (None, _ROPE_H, None)
    else:
        BLOCK_KIND[_b] = (None, None, None)


def _seg_block(tiles, name):
    b = SEG[name]
    t, r = divmod(b, COL_TILE // LANES)
    return tiles.index(t) * (COL_TILE // LANES) + r


def _params(dims, vmem_mb, flags=None):
    return pltpu.CompilerParams(dimension_semantics=dims, vmem_limit_bytes=vmem_mb * 1024 * 1024, flags=flags)


def _rms(x, g):
    ms = jnp.mean(x * x, axis=-1, keepdims=True)
    return x * lax.rsqrt(ms + NORM_EPS) * g


def _dot(a, b):
    return jnp.dot(a, b, preferred_element_type=F32)


def _dot_t(a, b):
    return lax.dot_general(a, b, (((1,), (1,)), ((), ())), preferred_element_type=F32)


def _dot_tn(a, b):
    return lax.dot_general(a, b, (((0,), (0,)), ((), ())), preferred_element_type=F32)


def _fold8(x, reduce_fn):
    return reduce_fn(x.reshape(x.shape[0] // SUBLANES, SUBLANES, x.shape[1]), axis=0)


def _staged_softmax(s_ref, p_ref, chunk, extra_logit=None):
    n_keys, n_q = s_ref.shape
    chunks = range(0, n_keys, chunk)
    m8 = functools.reduce(jnp.maximum, [_fold8(s_ref[c0:c0 + chunk, :], jnp.max) for c0 in chunks])
    m = jnp.max(m8, axis=0, keepdims=True)
    if extra_logit is not None:
        m = jnp.maximum(m, extra_logit)
    den8 = jnp.zeros((SUBLANES, n_q), F32)
    for c0 in chunks:
        e = jnp.exp2(s_ref[c0:c0 + chunk, :] - m)
        den8 = den8 + _fold8(e, jnp.sum)
        p_ref[c0:c0 + chunk, :] = e.astype(BF16)
    den = jnp.sum(den8, axis=0, keepdims=True)
    if extra_logit is not None:
        den = den + jnp.exp2(extra_logit - m)
    return den


def _colmax(parts):
    return functools.reduce(jnp.maximum, [jnp.max(p, axis=0, keepdims=True) for p in parts])


def _colsum(parts):
    return functools.reduce(jnp.add, [jnp.sum(p, axis=0, keepdims=True) for p in parts])


MOD_TN = 1024


def _mod_kernel(c_ref, w_ref, b_ref, o_ref):
    c = c_ref[...]
    cond = c * (1.0 / (1.0 + jnp.exp(-c)))
    o_ref[0] = _dot(cond.astype(BF16), w_ref[0].astype(BF16)) + b_ref[0]


def _mod_call(c_rows, w_mod, b_mod):
    n = N_MOD * D_MODEL
    return pl.pallas_call(
        _mod_kernel,
        grid=(DEPTH, n // MOD_TN),
        in_specs=[
            pl.BlockSpec((MOD_ROWS, D_MODEL), lambda l, j: (0, 0)),
            pl.BlockSpec((1, D_MODEL, MOD_TN), lambda l, j: (l, 0, j)),
            pl.BlockSpec((1, 1, MOD_TN), lambda l, j: (l, 0, j)),
        ],
        out_specs=pl.BlockSpec((1, MOD_ROWS, MOD_TN), lambda l, j: (l, 0, j)),
        out_shape=jax.ShapeDtypeStruct((DEPTH, MOD_ROWS, n), F32),
        compiler_params=_params(("arbitrary", "arbitrary"), 40),
        name="mod",
    )(c_rows, w_mod, b_mod.reshape(DEPTH, 1, n))


PROJ_TM = 512


def _proj_kernel(x_ref, shift_ref, scale_ref, g_ref, w_ref, rope_ref, hg_ref, *rest, tiles, n_cast):
    cast_src, o_ref, cast_dst = rest[:n_cast], rest[n_cast], rest[n_cast + 1:]
    for src, dst in zip(cast_src, cast_dst):
        dst[...] = src[...].astype(BF16)
    h = _rms(x_ref[...], g_ref[...]) * (1.0 + scale_ref[0]) + shift_ref[0]
    hb = h.astype(BF16)
    per = COL_TILE // LANES
    for n, t in enumerate(tiles):
        y = _dot(hb, w_ref[:, t * COL_TILE:(t + 1) * COL_TILE])
        for j in range(per):
            gain_row, rope, q_scale = BLOCK_KIND[t * per + j]
            yj = y[:, j * LANES:(j + 1) * LANES]
            if gain_row is not None:
                yj = _rms(yj, hg_ref[gain_row:gain_row + 1, :])
            if rope is not None:
                base = rope * 3 * LANES
                cos = rope_ref[:, base:base + LANES]
                sin_next = rope_ref[:, base + LANES:base + 2 * LANES]
                sin_prev = rope_ref[:, base + 2 * LANES:base + 3 * LANES]
                yj = (yj * cos + pltpu.roll(yj, LANES - 1, 1) * sin_next
                      + pltpu.roll(yj, 1, 1) * sin_prev)
            if q_scale is not None:
                yj = yj * q_scale
            o_ref[:, (n * per + j) * LANES:(n * per + j + 1) * LANES] = yj.astype(BF16)


def _proj_call(x2d, mod, g, w_in, layer, rope, head_gains, *, tiles, mod_row, rope_tile, cast=()):
    t_tok = x2d.shape[0]
    tm = PROJ_TM
    wout = len(tiles) * COL_TILE
    n_steps = t_tok // tm
    cast_in_specs, cast_out_specs, cast_shapes = [], [], []
    for w in cast:
        rows, cols = w.shape[1] // n_steps, w.shape[2]
        assert rows * n_steps == w.shape[1] and rows % 16 == 0
        cast_in_specs.append(pl.BlockSpec((None, rows, cols), lambda i: (layer, i, 0)))
        cast_out_specs.append(pl.BlockSpec((rows, cols), lambda i: (i, 0)))
        cast_shapes.append(jax.ShapeDtypeStruct(w.shape[1:], BF16))
    outs = pl.pallas_call(
        functools.partial(_proj_kernel, tiles=tiles, n_cast=len(cast)),
        grid=(n_steps,),
        in_specs=[
            pl.BlockSpec((tm, D_MODEL), lambda i: (i, 0)),
            pl.BlockSpec((None, 1, 1, D_MODEL), lambda i: (layer, mod_row(i) * N_MOD + 0, 0, 0)),
            pl.BlockSpec((None, 1, 1, D_MODEL), lambda i: (layer, mod_row(i) * N_MOD + 1, 0, 0)),
            pl.BlockSpec((None, 1, D_MODEL), lambda i: (layer, 0, 0)),
            pl.BlockSpec((None, D_MODEL, PROJ_WIDTH), lambda i: (layer, 0, 0), pipeline_mode=pl.Buffered(1)),
            pl.BlockSpec((tm, 6 * LANES), lambda i: (rope_tile(i), 0)),
            pl.BlockSpec((None, 8, LANES), lambda i: (layer, 0, 0)),
            *cast_in_specs,
        ],
        out_specs=[pl.BlockSpec((tm, wout), lambda i: (i, 0)), *cast_out_specs],
        out_shape=[jax.ShapeDtypeStruct((t_tok, wout), BF16), *cast_shapes],
        compiler_params=_params(("arbitrary",), 60),
        name="proj",
    )(x2d, mod, mod, g, w_in, rope, head_gains, *cast)
    return outs[0], outs[1:]


A_TQ = 512
KEY_CHUNK = 256


def _attn_a_kernel(q_ref, k_ref, v_ref, kc_ref, vc_ref, o_ref, kall_ref, vall_ref, s_ref, p_ref):
    kall_ref[:SEQ, :] = k_ref[0]
    kall_ref[SEQ:, :] = kc_ref[0]
    vall_ref[:SEQ, :] = v_ref[0]
    vall_ref[SEQ:, :] = vc_ref[0]
    units = [(r0, g) for r0 in range(0, SEQ, A_TQ) for g in range(2)]

    def scores(n):
        r0, g = units[n]
        s_ref[n % 2] = _dot_t(kall_ref[...], q_ref[0, r0:r0 + A_TQ, g * LANES:(g + 1) * LANES])

    scores(0)
    for n, (r0, g) in enumerate(units):
        if n + 1 < len(units):
            scores(n + 1)
        den = _staged_softmax(s_ref.at[n % 2], p_ref, KEY_CHUNK)
        o_t = _dot_tn(vall_ref[...], p_ref[...])
        o_ref[0, r0:r0 + A_TQ, g * LANES:(g + 1) * LANES] = (o_t / den).T.astype(BF16)


def _attn_a_call(px, pc, ctx_tiles):
    b = px.shape[0]
    ak, av = SEG["ak"], SEG["av"]
    akc, avc = _seg_block(ctx_tiles, "ak"), _seg_block(ctx_tiles, "av")
    return pl.pallas_call(
        _attn_a_kernel,
        grid=(b, 2),
        in_specs=[
            pl.BlockSpec((1, SEQ, 2 * LANES), lambda i, h: (i, 0, h)),
            pl.BlockSpec((1, SEQ, LANES), lambda i, h: (i, 0, ak + h)),
            pl.BlockSpec((1, SEQ, LANES), lambda i, h: (i, 0, av + h)),
            pl.BlockSpec((1, CTX_LEN, LANES), lambda i, h: (i, 0, akc + h)),
            pl.BlockSpec((1, CTX_LEN, LANES), lambda i, h: (i, 0, avc + h)),
        ],
        out_specs=pl.BlockSpec((1, SEQ, 2 * LANES), lambda i, h: (i, 0, h)),
        out_shape=jax.ShapeDtypeStruct((b, SEQ, GROUP_WIDTH), BF16),
        scratch_shapes=[pltpu.VMEM((SEQ + CTX_LEN, LANES), BF16), pltpu.VMEM((SEQ + CTX_LEN, LANES), BF16),
                        pltpu.VMEM((2, SEQ + CTX_LEN, A_TQ), F32), pltpu.VMEM((SEQ + CTX_LEN, A_TQ), BF16)],
        compiler_params=_params(("arbitrary",) * 2, 48),
        name="attn_a",
    )(px, px, px, pc, pc)


NB_ROWS = SEQ // GRID_W
NB_QROWS = 4
NB_KROWS = NB_QROWS + WIN_ROWS - 1
NB_BLOCKS = NB_ROWS // NB_QROWS
NB_TQ = NB_QROWS * GRID_W
NB_KEYS = NB_KROWS * GRID_W
NB_CHUNK = (NB_KEYS + CTX_LEN) // 3


def _nb_window_start(j):
    return min(max(j * NB_QROWS - WIN_ROWS // 2, 0), NB_ROWS - NB_KROWS)


def _nb_class(j):
    return 0 if j == 0 else (2 if j == NB_BLOCKS - 1 else 1)


def _na_bias_table(rpb):
    n_dr, n_dc = rpb.shape[-2:]
    rpb = rpb.astype(F32).reshape(-1, n_dr, n_dc)
    n_tab = rpb.shape[0]
    period = 2 * GRID_W
    fill = jnp.full((n_tab, n_dr, period - n_dc), NEG_INF, F32)
    ext = jnp.concatenate([rpb[..., WIN_COLS - 1::-1], fill, rpb[..., :WIN_COLS - 1:-1]], axis=-1)
    flat = jnp.tile(ext, (1, 1, GRID_W))[..., :GRID_W * (period - 1)]
    toep = flat.reshape(n_tab, n_dr, GRID_W, period - 1)[..., :GRID_W]
    c = jnp.arange(GRID_W)
    c_start = jnp.clip(c - WIN_COLS // 2, 0, GRID_W - WIN_COLS)
    kc = jnp.arange(GRID_W)
    in_win = (kc[:, None] >= c_start[None, :]) & (kc[:, None] < c_start[None, :] + WIN_COLS)
    toep = jnp.where(in_win, toep * LOG2E, NEG_INF)

    def masked_rows(n):
        return jnp.full((n_tab, n, GRID_W, GRID_W), NEG_INF, F32)

    classes = []
    for j in (0, 1, NB_BLOCKS - 1):
        q_rows = []
        for rq in range(NB_QROWS):
            r = j * NB_QROWS + rq
            r_start = min(max(r - WIN_ROWS // 2, 0), NB_ROWS - WIN_ROWS)
            lo = r_start - _nb_window_start(j)
            drow0 = r_start - r + WIN_ROWS - 1
            pieces = [masked_rows(lo), toep[:, drow0:drow0 + WIN_ROWS], masked_rows(NB_KROWS - WIN_ROWS - lo)]
            rows = jnp.concatenate([p for p in pieces if p.shape[1]], axis=1)
            q_rows.append(rows.reshape(n_tab, NB_KEYS, GRID_W))
        classes.append(jnp.concatenate(q_rows, axis=-1))
    return jnp.stack(classes, axis=1)


def _attn_b_kernel(q_ref, k_ref, v_ref, kc_ref, vc_ref, bias_ref, o_ref, s_ref, p_ref):
    def scores(j):
        q0, k0 = j * NB_TQ, _nb_window_start(j) * GRID_W
        q = q_ref[0, q0:q0 + NB_TQ, :]
        s_ref[j % 2, :NB_KEYS, :] = _dot_t(k_ref[0, k0:k0 + NB_KEYS, :], q) + bias_ref[0, _nb_class(j)]
        s_ref[j % 2, NB_KEYS:, :] = _dot_t(kc_ref[0], q)

    scores(0)
    for j in range(NB_BLOCKS):
        if j + 1 < NB_BLOCKS:
            scores(j + 1)
        q0, k0 = j * NB_TQ, _nb_window_start(j) * GRID_W
        den = _staged_softmax(s_ref.at[j % 2], p_ref, NB_CHUNK)
        o_t = (_dot_tn(v_ref[0, k0:k0 + NB_KEYS, :], p_ref[:NB_KEYS, :])
               + _dot_tn(vc_ref[0], p_ref[NB_KEYS:, :])) / den
        o_ref[0, q0:q0 + NB_TQ, :] = o_t.T.astype(BF16)


def _attn_b_call(px, pc, ctx_tiles, bias, layer):
    b = px.shape[0]
    bq, bk, bv = SEG["bq"], SEG["bk"], SEG["bv"]
    bkc, bvc = _seg_block(ctx_tiles, "bk"), _seg_block(ctx_tiles, "bv")
    return pl.pallas_call(
        _attn_b_kernel,
        grid=(b, 4),
        in_specs=[
            pl.BlockSpec((1, SEQ, LANES), lambda i, h: (i, 0, bq + h)),
            pl.BlockSpec((1, SEQ, LANES), lambda i, h: (i, 0, bk + h)),
            pl.BlockSpec((1, SEQ, LANES), lambda i, h: (i, 0, bv + h)),
            pl.BlockSpec((1, CTX_LEN, LANES), lambda i, h: (i, 0, bkc + h)),
            pl.BlockSpec((1, CTX_LEN, LANES), lambda i, h: (i, 0, bvc + h)),
            pl.BlockSpec((1, 3, NB_KEYS, NB_TQ), lambda i, h: (layer * 4 + h, 0, 0, 0)),
        ],
        out_specs=pl.BlockSpec((1, SEQ, LANES), lambda i, h: (i, 0, h)),
        out_shape=jax.ShapeDtypeStruct((b, SEQ, GROUP_WIDTH), BF16),
        scratch_shapes=[pltpu.VMEM((2, NB_KEYS + CTX_LEN, NB_TQ), F32),
                        pltpu.VMEM((NB_KEYS + CTX_LEN, NB_TQ), BF16)],
        compiler_params=_params(("arbitrary",) * 2, 32),
        name="attn_b",
    )(px, px, px, pc, pc, bias)


C_TQ = 512


def _diff_lambda(lam_ref, lam_init):
    a = jnp.sum(lam_ref[0:1, :] * lam_ref[1:2, :], axis=-1, keepdims=True)
    b = jnp.sum(lam_ref[2:3, :] * lam_ref[3:4, :], axis=-1, keepdims=True)
    return jnp.exp(a) - jnp.exp(b) + lam_init


def _diff_head(q, keys, vals, lam, gsub, lam_init):
    lane = lax.broadcasted_iota(jnp.int32, q.shape, 1)
    zero = jnp.zeros_like(q)
    maps = []
    for sub in range(2):
        qs = jnp.where((lane < DIFF_QK_DIM) == (sub == 0), q, zero)
        s = [_dot_t(k, qs) for k in keys]
        m = _colmax(s)
        e = [jnp.exp2(x - m) for x in s]
        maps.append((e, _colsum(e)))
    (e1, den1), (e2, den2) = maps
    ratio = lam * den1 / den2
    o_t = functools.reduce(jnp.add, [
        _dot_tn(v, (a - ratio * b).astype(BF16)) for a, b, v in zip(e1, e2, vals)]) / den1
    return _rms(o_t.T, gsub) * (1.0 - lam_init)


def _attn_c_kernel(lam_ref, gsub_ref, q_ref, k_ref, v_ref, kc_ref, vc_ref, o_ref,
                   kall_ref, vall_ref, s_ref, p_ref, *, lam_init):
    n_keys = SEQ + CTX_LEN
    kall_ref[:SEQ, :] = k_ref[0]
    kall_ref[SEQ:, :] = kc_ref[0]
    vall_ref[:SEQ, :] = v_ref[0]
    vall_ref[SEQ:, :] = vc_ref[0]
    lam = _diff_lambda(lam_ref, lam_init)
    lane = lax.broadcasted_iota(jnp.int32, (C_TQ, LANES), 1)
    tiles = list(range(0, SEQ, C_TQ))
    chunks = range(0, n_keys, KEY_CHUNK)

    def scores(n):
        q = q_ref[0, tiles[n]:tiles[n] + C_TQ, :]
        for sub in range(2):
            qs = jnp.where((lane < DIFF_QK_DIM) == (sub == 0), q, jnp.zeros_like(q))
            s_ref[n % 2, sub] = _dot_t(kall_ref[...], qs)

    scores(0)
    for n, r0 in enumerate(tiles):
        if n + 1 < len(tiles):
            scores(n + 1)
        slot = n % 2
        dens = []
        for sub in range(2):
            m8 = functools.reduce(jnp.maximum,
                                  [_fold8(s_ref[slot, sub, c0:c0 + KEY_CHUNK, :], jnp.max) for c0 in chunks])
            m = jnp.max(m8, axis=0, keepdims=True)
            den8 = jnp.zeros((SUBLANES, C_TQ), F32)
            for c0 in chunks:
                e = jnp.exp2(s_ref[slot, sub, c0:c0 + KEY_CHUNK, :] - m)
                den8 = den8 + _fold8(e, jnp.sum)
                s_ref[slot, sub, c0:c0 + KEY_CHUNK, :] = e
            dens.append(jnp.sum(den8, axis=0, keepdims=True))
        den1, den2 = dens
        ratio = lam * den1 / den2
        for c0 in chunks:
            p_ref[c0:c0 + KEY_CHUNK, :] = (s_ref[slot, 0, c0:c0 + KEY_CHUNK, :]
                                           - ratio * s_ref[slot, 1, c0:c0 + KEY_CHUNK, :]).astype(BF16)
        o_t = _dot_tn(vall_ref[...], p_ref[...]) / den1
        o = _rms(o_t.T, gsub_ref[...]) * (1.0 - lam_init)
        o_ref[0, r0:r0 + C_TQ, :] = o.astype(BF16)


def _attn_c_call(px, pc, ctx_tiles, lam_rows, gsub, layer, lam_init):
    b = px.shape[0]
    cq, ck, cv = SEG["cq"], SEG["ck"], SEG["cv"]
    ckc, cvc = _seg_block(ctx_tiles, "ck"), _seg_block(ctx_tiles, "cv")
    return pl.pallas_call(
        functools.partial(_attn_c_kernel, lam_init=lam_init),
        grid=(b, 4),
        in_specs=[
            pl.BlockSpec((None, 8, LANES), lambda i, h: (layer, 0, 0)),
            pl.BlockSpec((None, 1, LANES), lambda i, h: (layer, 0, 0)),
            pl.BlockSpec((1, SEQ, LANES), lambda i, h: (i, 0, cq + h)),
            pl.BlockSpec((1, SEQ, LANES), lambda i, h: (i, 0, ck + h)),
            pl.BlockSpec((1, SEQ, LANES), lambda i, h: (i, 0, cv + h)),
            pl.BlockSpec((1, CTX_LEN, LANES), lambda i, h: (i, 0, ckc + h)),
            pl.BlockSpec((1, CTX_LEN, LANES), lambda i, h: (i, 0, cvc + h)),
        ],
        out_specs=pl.BlockSpec((1, SEQ, LANES), lambda i, h: (i, 0, h)),
        out_shape=jax.ShapeDtypeStruct((b, SEQ, GROUP_WIDTH), BF16),
        scratch_shapes=[pltpu.VMEM((SEQ + CTX_LEN, LANES), BF16), pltpu.VMEM((SEQ + CTX_LEN, LANES), BF16),
                        pltpu.VMEM((2, 2, SEQ + CTX_LEN, C_TQ), F32), pltpu.VMEM((SEQ + CTX_LEN, C_TQ), BF16)],
        compiler_params=_params(("arbitrary",) * 2, 48),
        name="attn_c",
    )(lam_rows, gsub, px, px, px, pc, pc)


D_TQ = 256
D_SPAN = D_TQ + 2 * WINDOW


def _attn_d_kernel(sink_ref, q_ref, k_ref, v_ref, kc_ref, vc_ref, o_ref, s_ref, p_ref, *, layer):
    kvh = pl.program_id(1)
    col = lax.broadcasted_iota(jnp.int32, (D_SPAN, 2 * D_TQ), 1)
    rel = (col & (D_TQ - 1)) - lax.broadcasted_iota(jnp.int32, (D_SPAN, 2 * D_TQ), 0)
    head_col = lax.broadcasted_iota(jnp.int32, (1, 2 * D_TQ), 1)
    sink = jnp.where(head_col < D_TQ, sink_ref[layer, kvh * 2], sink_ref[layer, kvh * 2 + 1]) * LOG2E
    n_blocks = SEQ // D_TQ

    def key_start(i):
        return min(max(i * D_TQ - WINDOW, 0), SEQ - D_SPAN)

    def scores(i):
        q0, k0 = i * D_TQ, key_start(i)
        q = jnp.concatenate([q_ref[0, q0:q0 + D_TQ, :LANES], q_ref[0, q0:q0 + D_TQ, LANES:]], axis=0)
        valid = jnp.abs(rel + (q0 - k0)) <= WINDOW
        s_ref[i % 2, :D_SPAN, :] = jnp.where(valid, _dot_t(k_ref[0, k0:k0 + D_SPAN, :], q), NEG_INF)
        s_ref[i % 2, D_SPAN:, :] = _dot_t(kc_ref[0], q)

    scores(0)
    for i in range(n_blocks):
        if i + 1 < n_blocks:
            scores(i + 1)
        q0, k0 = i * D_TQ, key_start(i)
        den = _staged_softmax(s_ref.at[i % 2], p_ref, KEY_CHUNK, extra_logit=sink)
        o_t = (_dot_tn(v_ref[0, k0:k0 + D_SPAN, :], p_ref[:D_SPAN, :])
               + _dot_tn(vc_ref[0], p_ref[D_SPAN:, :])) / den
        o_ref[0, q0:q0 + D_TQ, :LANES] = o_t[:, :D_TQ].T.astype(BF16)
        o_ref[0, q0:q0 + D_TQ, LANES:] = o_t[:, D_TQ:].T.astype(BF16)


def _attn_d_call(px, pc, ctx_tiles, sink, layer):
    b = px.shape[0]
    dq, dk, dv = SEG["dq"], SEG["dk"], SEG["dv"]
    dkc, dvc = _seg_block(ctx_tiles, "dk"), _seg_block(ctx_tiles, "dv")
    return pl.pallas_call(
        functools.partial(_attn_d_kernel, layer=layer),
        grid=(b, 2),
        in_specs=[
            pl.BlockSpec(memory_space=pltpu.SMEM),
            pl.BlockSpec((1, SEQ, 2 * LANES), lambda i, h: (i, 0, dq // 2 + h)),
            pl.BlockSpec((1, SEQ, LANES), lambda i, h: (i, 0, dk + h)),
            pl.BlockSpec((1, SEQ, LANES), lambda i, h: (i, 0, dv + h)),
            pl.BlockSpec((1, CTX_LEN, LANES), lambda i, h: (i, 0, dkc + h)),
            pl.BlockSpec((1, CTX_LEN, LANES), lambda i, h: (i, 0, dvc + h)),
        ],
        out_specs=pl.BlockSpec((1, SEQ, 2 * LANES), lambda i, h: (i, 0, h)),
        out_shape=jax.ShapeDtypeStruct((b, SEQ, GROUP_WIDTH), BF16),
        scratch_shapes=[pltpu.VMEM((2, D_SPAN + CTX_LEN, 2 * D_TQ), F32),
                        pltpu.VMEM((D_SPAN + CTX_LEN, 2 * D_TQ), BF16)],
        compiler_params=_params(("arbitrary",) * 2, 32),
        name="attn_d",
    )(sink, px, px, px, pc, pc)


def _softmax_pv(q, k, v, sink=None):
    s = _dot_t(q, k)
    m = jnp.max(s, axis=-1, keepdims=True)
    if sink is not None:
        m = jnp.maximum(m, sink)
    e = jnp.exp2(s - m)
    den = jnp.sum(e, axis=-1, keepdims=True)
    if sink is not None:
        den = den + jnp.exp2(sink - m)
    return _dot(e.astype(BF16), v) / den


def _ctx_attn_kernel(sink_ref, lam_ref, gsub_ref, p_ref, o_ref, *, layer, lam_init):
    def blk(name, h):
        c = (SEG[name] + h) * LANES
        return p_ref[0, :, c:c + LANES]

    def put(group, h, o):
        c = group * GROUP_WIDTH + h * LANES
        o_ref[0, :, c:c + LANES] = o.astype(BF16)

    lam = _diff_lambda(lam_ref, lam_init)
    for h in range(4):
        put(0, h, _softmax_pv(blk("aq", h), blk("ak", h // 2), blk("av", h // 2)))
        put(1, h, _softmax_pv(blk("bq", h), blk("bk", h), blk("bv", h)))
        put(2, h, _diff_head(blk("cq", h), [blk("ck", h)], [blk("cv", h)], lam, gsub_ref[...], lam_init))
        put(3, h, _softmax_pv(blk("dq", h), blk("dk", h // 2), blk("dv", h // 2),
                              sink=sink_ref[layer, h] * LOG2E))


def _ctx_attn_call(pc, sink, lam_rows, gsub, layer, lam_init):
    b = pc.shape[0]
    return pl.pallas_call(
        functools.partial(_ctx_attn_kernel, layer=layer, lam_init=lam_init),
        grid=(b,),
        in_specs=[
            pl.BlockSpec(memory_space=pltpu.SMEM),
            pl.BlockSpec((None, 8, LANES), lambda i: (layer, 0, 0)),
            pl.BlockSpec((None, 1, LANES), lambda i: (layer, 0, 0)),
            pl.BlockSpec((1, CTX_LEN, PROJ_WIDTH), lambda i: (i, 0, 0)),
        ],
        out_specs=pl.BlockSpec((1, CTX_LEN, D_MODEL), lambda i: (i, 0, 0)),
        out_shape=jax.ShapeDtypeStruct((b, CTX_LEN, D_MODEL), BF16),
        compiler_params=_params(("arbitrary",), 32),
        name="ctx_attn",
    )(sink, lam_rows, gsub, pc)


POST_TM = 512
POST_TH = 1024


def _post_kernel(x_ref, oa_ref, ob_ref, oc_ref, od_ref, wout_ref, gate_mix_ref, shift_ref, scale_ref,
                 gate_mlp_ref, g_ref, wup_ref, wdn_ref, gfin_ref, out_ref, x1_ref, h_ref, *, final):
    j = pl.program_id(1)

    @pl.when(j == 0)
    def _():
        mix = functools.reduce(jnp.add, [
            _dot(o[...], wout_ref[n * GROUP_WIDTH:(n + 1) * GROUP_WIDTH, :])
            for n, o in enumerate((oa_ref, ob_ref, oc_ref, od_ref))])
        x1 = x_ref[...] + gate_mix_ref[0] * mix
        x1_ref[...] = x1
        h_ref[...] = (_rms(x1, g_ref[...]) * (1.0 + scale_ref[0]) + shift_ref[0]).astype(BF16)
        out_ref[...] = jnp.zeros_like(out_ref)

    u = _dot(h_ref[...], wup_ref[...])
    act = jnp.square(jnp.maximum(u, 0.0)).astype(BF16)
    out_ref[...] += _dot(act, wdn_ref[...])

    @pl.when(j == pl.num_programs(1) - 1)
    def _():
        y = x1_ref[...] + gate_mlp_ref[0] * out_ref[...]
        if final:
            y = _rms(y, gfin_ref[...])
        out_ref[...] = y


def _post_call(x2d, mix_parts, mix_blocks, w_out, mod, g_mlp, w_up, w_down, layer, g_final, *,
               mod_row, final):
    t_tok = x2d.shape[0]
    tm, th = POST_TM, POST_TH
    once = pl.Buffered(1)

    def mod_spec(k):
        return pl.BlockSpec((None, 1, 1, D_MODEL), lambda i, j: (layer, mod_row(i) * N_MOD + k, 0, 0))

    def part_spec(blk):
        return pl.BlockSpec((tm, GROUP_WIDTH), lambda i, j: (i, blk))

    return pl.pallas_call(
        functools.partial(_post_kernel, final=final),
        grid=(t_tok // tm, MLP_HIDDEN // th),
        in_specs=[
            pl.BlockSpec((tm, D_MODEL), lambda i, j: (i, 0)),
            *[part_spec(blk) for blk in mix_blocks],
            pl.BlockSpec((D_MODEL, D_MODEL), lambda i, j: (0, 0), pipeline_mode=once),
            mod_spec(2), mod_spec(3), mod_spec(4), mod_spec(5),
            pl.BlockSpec((None, 1, D_MODEL), lambda i, j: (layer, 0, 0)),
            pl.BlockSpec((D_MODEL, th), lambda i, j: (0, j)),
            pl.BlockSpec((th, D_MODEL), lambda i, j: (j, 0)),
            pl.BlockSpec((1, D_MODEL), lambda i, j: (0, 0)),
        ],
        out_specs=pl.BlockSpec((tm, D_MODEL), lambda i, j: (i, 0)),
        out_shape=jax.ShapeDtypeStruct((t_tok, D_MODEL), F32),
        scratch_shapes=[pltpu.VMEM((tm, D_MODEL), F32), pltpu.VMEM((tm, D_MODEL), BF16)],
        compiler_params=_params(("arbitrary", "arbitrary"), 60),
        name="post",
    )(x2d, *mix_parts, w_out, mod, mod, mod, mod, g_mlp, w_up, w_down, g_final)


def _rope_rows(n_tok, dim):
    t = jnp.arange(n_tok)
    row = (t // GRID_W).astype(F32)
    col = (t % GRID_W).astype(F32)
    n_freq = dim // 4
    inv_freq = ROPE_THETA ** (-jnp.arange(n_freq, dtype=F32) / n_freq)
    ang = jnp.concatenate([row[:, None] * inv_freq, col[:, None] * inv_freq], axis=-1)
    cos = jnp.repeat(jnp.cos(ang), 2, axis=-1)
    sin = jnp.repeat(jnp.sin(ang), 2, axis=-1)
    even = (jnp.arange(dim) % 2) == 0
    parts = [cos, jnp.where(even, -sin, 0.0), jnp.where(even, 0.0, sin)]
    return [jnp.tile(p, (1, LANES // dim)) for p in parts]


def _rope_table(n_tok):
    return jnp.concatenate(_rope_rows(n_tok, HEAD_DIM) + _rope_rows(n_tok, DIFF_QK_DIM), axis=-1)


def _identity_rope_table(n_tok):
    one = jnp.ones((n_tok, LANES), F32)
    zero = jnp.zeros((n_tok, LANES), F32)
    return jnp.concatenate([one, zero, zero] * 2, axis=-1)


def _pad_rows(rows, n_rows=8):
    depth = rows[0].shape[0]
    padded = [jnp.pad(r.astype(F32), ((0, 0), (0, LANES - r.shape[1]))) for r in rows]
    padded += [jnp.zeros((depth, LANES), F32)] * (n_rows - len(rows))
    return jnp.stack(padded, axis=1)


def kernel(x, c, ctx, c_ctx, g_mix, g_mlp, w_mod, b_mod, w_in, w_out, gqa_gq, gqa_gk, na_rpb,
           diff_lq1, diff_lk1, diff_lq2, diff_lk2, diff_gsub, swa_sink, w_up, w_down, g_final):
    b, s, d = x.shape
    assert (s, d) == (SEQ, D_MODEL) and ctx.shape == (b, CTX_LEN, D_MODEL) and b <= CTX_MOD_ROW

    c_rows = jnp.concatenate(
        [c, jnp.zeros((CTX_MOD_ROW - b, D_MODEL), F32), c_ctx[None],
         jnp.zeros((MOD_ROWS - CTX_MOD_ROW - 1, D_MODEL), F32)], axis=0)
    mod = _mod_call(c_rows, w_mod, b_mod).reshape(DEPTH, MOD_ROWS * N_MOD, 1, D_MODEL)

    w_in_b = w_in.astype(BF16)

    rope_x = _rope_table(SEQ)
    rope_c = _identity_rope_table(PROJ_TM)
    x_tiles_per_seq = SEQ // PROJ_TM
    x_post_tiles_per_seq = SEQ // POST_TM
    g_final2 = g_final.reshape(1, D_MODEL)
    g_mix3 = g_mix.reshape(DEPTH, 1, D_MODEL)
    g_mlp3 = g_mlp.reshape(DEPTH, 1, D_MODEL)
    head_gains = _pad_rows([gqa_gq, gqa_gk])
    lam_rows = _pad_rows([diff_lq1, diff_lk1, diff_lq2, diff_lk2])
    gsub = diff_gsub.reshape(DEPTH, 1, HEAD_DIM)
    sink = swa_sink.astype(F32)
    nb_bias = _na_bias_table(na_rpb)

    x2d = x.reshape(b * s, D_MODEL)
    c2d = ctx.reshape(b * CTX_LEN, D_MODEL)
    for l in range(DEPTH):
        with_ctx = l < DEPTH - 1
        lam_init = 0.8 - 0.6 * math.exp(-0.3 * l)
        ctx_tiles = ALL_TILES if with_ctx else KV_TILES

        px, (w_out_b, w_up_b, w_down_b) = _proj_call(
            x2d, mod, g_mix3, w_in_b, l, rope_x, head_gains, tiles=ALL_TILES, cast=(w_out, w_up, w_down),
            mod_row=lambda i: i // x_tiles_per_seq, rope_tile=lambda i: i % x_tiles_per_seq)
        pc, _ = _proj_call(c2d, mod, g_mix3, w_in_b, l, rope_c, head_gains, tiles=ctx_tiles,
                           mod_row=lambda i: CTX_MOD_ROW, rope_tile=lambda i: 0)
        px = px.reshape(b, SEQ, PROJ_WIDTH)
        pc = pc.reshape(b, CTX_LEN, len(ctx_tiles) * COL_TILE)

        oa = _attn_a_call(px, pc, ctx_tiles)
        ob = _attn_b_call(px, pc, ctx_tiles, nb_bias, l)
        oc = _attn_c_call(px, pc, ctx_tiles, lam_rows, gsub, l, lam_init)
        od = _attn_d_call(px, pc, ctx_tiles, sink, l)
        parts = [o.reshape(b * s, GROUP_WIDTH) for o in (oa, ob, oc, od)]
        if with_ctx:
            o_ctx = _ctx_attn_call(pc, sink, lam_rows, gsub, l, lam_init).reshape(b * CTX_LEN, D_MODEL)

        x2d = _post_call(x2d, parts, (0, 0, 0, 0), w_out_b, mod, g_mlp3, w_up_b, w_down_b, l, g_final2,
                         mod_row=lambda i: i // x_post_tiles_per_seq, final=not with_ctx)
        if with_ctx:
            c2d = _post_call(c2d, [o_ctx] * 4, (0, 1, 2, 3), w_out_b, mod, g_mlp3, w_up_b, w_down_b, l,
                             g_final2, mod_row=lambda i: CTX_MOD_ROW, final=False)
    return x2d.reshape(b, s, D_MODEL)
```

```python
import functools
import math

import jax
import jax.numpy as jnp
from jax import lax
from jax.experimental import pallas as pl
from jax.experimental.pallas import tpu as pltpu

D_MODEL = 2048
SEQ = 2048
DEPTH = 2
GRID_W = 64
CTX_LEN = 256
HEAD_DIM = 128
GROUP_WIDTH = 512
WIN_ROWS = 8
WIN_COLS = 16
DIFF_QK_DIM = 64
WINDOW = 128
MLP_HIDDEN = 4 * D_MODEL
N_MOD = 6
ROPE_THETA = 10000.0
NORM_EPS = 1e-6
NEG_INF = -1e30
PROJ_WIDTH = 5120
LOG2E = math.log2(math.e)

LANES = 128
SUBLANES = 8
BF16_ROWS = 16
COL_TILE = 512
MOD_ROWS = 16
CTX_MOD_ROW = 8

F32 = jnp.float32
BF16 = jnp.bfloat16

SEG = dict(aq=0, ak=4, av=6, bq=8, bk=12, bv=16, cq=20, ck=24, cv=28, dq=32, dk=36, dv=38)
ALL_TILES = tuple(range(PROJ_WIDTH // COL_TILE))
KV_TILES = (1, 3, 4, 6, 7, 9)
Q_SCALE_H = HEAD_DIM ** -0.5 * LOG2E
Q_SCALE_D = DIFF_QK_DIM ** -0.5 * LOG2E
_ROPE_H, _ROPE_D = 0, 1
BLOCK_KIND = {}
for _b in range(PROJ_WIDTH // LANES):
    if _b < 4:
        BLOCK_KIND[_b] = (0, _ROPE_H, Q_SCALE_H)
    elif _b < 6:
        BLOCK_KIND[_b] = (1, _ROPE_H, None)
    elif 8 <= _b < 12:
        BLOCK_KIND[_b] = (None, None, Q_SCALE_H)
    elif 20 <= _b < 24:
        BLOCK_KIND[_b] = (None, _ROPE_D, Q_SCALE_D)
    elif 24 <= _b < 28:
        BLOCK_KIND[_b] = (None, _ROPE_D, None)
    elif 32 <= _b < 36:
        BLOCK_KIND[_b] = (None, _ROPE_H, Q_SCALE_H)
    elif 36 <= _b < 38:
        BLOCK_KIND[_b] = (None, _ROPE_H, None)
    else:
        BLOCK_KIND[_b] = (None, None, None)


def _seg_block(tiles, name):
    b = SEG[name]
    t, r = divmod(b, COL_TILE // LANES)
    return tiles.index(t) * (COL_TILE // LANES) + r


def _params(dims, vmem_mb):
    return pltpu.CompilerParams(dimension_semantics=dims, vmem_limit_bytes=vmem_mb * 1024 * 1024)


def _rms(x, g):
    ms = jnp.mean(x * x, axis=-1, keepdims=True)
    return x * lax.rsqrt(ms + NORM_EPS) * g


def _dot(a, b):
    return jnp.dot(a, b, preferred_element_type=F32)


def _dot_t(a, b):
    return lax.dot_general(a, b, (((1,), (1,)), ((), ())), preferred_element_type=F32)


def _dot_tn(a, b):
    return lax.dot_general(a, b, (((0,), (0,)), ((), ())), preferred_element_type=F32)


def _fold8(x, reduce_fn):
    return reduce_fn(x.reshape(x.shape[0] // SUBLANES, SUBLANES, x.shape[1]), axis=0)


def _staged_softmax(s_ref, p_ref, chunk, extra_logit=None):
    n_keys, n_q = s_ref.shape
    chunks = range(0, n_keys, chunk)
    m8 = functools.reduce(jnp.maximum, [_fold8(s_ref[c0:c0 + chunk, :], jnp.max) for c0 in chunks])
    m = jnp.max(m8, axis=0, keepdims=True)
    if extra_logit is not None:
        m = jnp.maximum(m, extra_logit)
    den8 = jnp.zeros((SUBLANES, n_q), F32)
    for c0 in chunks:
        e = jnp.exp2(s_ref[c0:c0 + chunk, :] - m)
        den8 = den8 + _fold8(e, jnp.sum)
        p_ref[c0:c0 + chunk, :] = e.astype(BF16)
    den = jnp.sum(den8, axis=0, keepdims=True)
    if extra_logit is not None:
        den = den + jnp.exp2(extra_logit - m)
    return den


def _rowmax(parts):
    return functools.reduce(jnp.maximum, [jnp.max(p, axis=-1, keepdims=True) for p in parts])


def _rowsum(parts):
    return functools.reduce(jnp.add, [jnp.sum(p, axis=-1, keepdims=True) for p in parts])


def _colmax(parts):
    return functools.reduce(jnp.maximum, [jnp.max(p, axis=0, keepdims=True) for p in parts])


def _colsum(parts):
    return functools.reduce(jnp.add, [jnp.sum(p, axis=0, keepdims=True) for p in parts])


MOD_TN = 1024


def _mod_kernel(c_ref, w_ref, b_ref, o_ref):
    c = c_ref[...]
    cond = c * (1.0 / (1.0 + jnp.exp(-c)))
    o_ref[0] = _dot(cond.astype(BF16), w_ref[0].astype(BF16)) + b_ref[0]


def _mod_call(c_rows, w_mod, b_mod):
    n = N_MOD * D_MODEL
    return pl.pallas_call(
        _mod_kernel,
        grid=(DEPTH, n // MOD_TN),
        in_specs=[
            pl.BlockSpec((MOD_ROWS, D_MODEL), lambda l, j: (0, 0)),
            pl.BlockSpec((1, D_MODEL, MOD_TN), lambda l, j: (l, 0, j)),
            pl.BlockSpec((1, 1, MOD_TN), lambda l, j: (l, 0, j)),
        ],
        out_specs=pl.BlockSpec((1, MOD_ROWS, MOD_TN), lambda l, j: (l, 0, j)),
        out_shape=jax.ShapeDtypeStruct((DEPTH, MOD_ROWS, n), F32),
        compiler_params=_params(("arbitrary", "arbitrary"), 40),
        name="mod",
    )(c_rows, w_mod, b_mod.reshape(DEPTH, 1, n))


PROJ_TM = 512


def _proj_kernel(x_ref, shift_ref, scale_ref, g_ref, w_ref, rope_ref, hg_ref, *rest, tiles, n_cast):
    cast_src, o_ref, cast_dst = rest[:n_cast], rest[n_cast], rest[n_cast + 1:]
    for src, dst in zip(cast_src, cast_dst):
        dst[...] = src[...].astype(BF16)
    h = _rms(x_ref[...], g_ref[...]) * (1.0 + scale_ref[0]) + shift_ref[0]
    hb = h.astype(BF16)
    per = COL_TILE // LANES
    for n, t in enumerate(tiles):
        y = _dot(hb, w_ref[:, t * COL_TILE:(t + 1) * COL_TILE])
        for j in range(per):
            gain_row, rope, q_scale = BLOCK_KIND[t * per + j]
            yj = y[:, j * LANES:(j + 1) * LANES]
            if gain_row is not None:
                yj = _rms(yj, hg_ref[gain_row:gain_row + 1, :])
            if rope is not None:
                base = rope * 3 * LANES
                cos = rope_ref[:, base:base + LANES]
                sin_next = rope_ref[:, base + LANES:base + 2 * LANES]
                sin_prev = rope_ref[:, base + 2 * LANES:base + 3 * LANES]
                yj = (yj * cos + pltpu.roll(yj, LANES - 1, 1) * sin_next
                      + pltpu.roll(yj, 1, 1) * sin_prev)
            if q_scale is not None:
                yj = yj * q_scale
            o_ref[:, (n * per + j) * LANES:(n * per + j + 1) * LANES] = yj.astype(BF16)


def _proj_call(x2d, mod, g, w_in, layer, rope, head_gains, *, tiles, mod_row, rope_tile, cast=()):
    t_tok = x2d.shape[0]
    tm = PROJ_TM
    wout = len(tiles) * COL_TILE
    n_steps = t_tok // tm
    cast_in_specs, cast_out_specs, cast_shapes = [], [], []
    for w in cast:
        rows, cols = w.shape[1] // n_steps, w.shape[2]
        assert rows * n_steps == w.shape[1] and rows % BF16_ROWS == 0
        cast_in_specs.append(pl.BlockSpec((None, rows, cols), lambda i: (layer, i, 0)))
        cast_out_specs.append(pl.BlockSpec((rows, cols), lambda i: (i, 0)))
        cast_shapes.append(jax.ShapeDtypeStruct(w.shape[1:], BF16))
    outs = pl.pallas_call(
        functools.partial(_proj_kernel, tiles=tiles, n_cast=len(cast)),
        grid=(n_steps,),
        in_specs=[
            pl.BlockSpec((tm, D_MODEL), lambda i: (i, 0)),
            pl.BlockSpec((None, 1, 1, D_MODEL), lambda i: (layer, mod_row(i) * N_MOD + 0, 0, 0)),
            pl.BlockSpec((None, 1, 1, D_MODEL), lambda i: (layer, mod_row(i) * N_MOD + 1, 0, 0)),
            pl.BlockSpec((None, 1, D_MODEL), lambda i: (layer, 0, 0)),
            pl.BlockSpec((None, D_MODEL, PROJ_WIDTH), lambda i: (layer, 0, 0), pipeline_mode=pl.Buffered(1)),
            pl.BlockSpec((tm, 6 * LANES), lambda i: (rope_tile(i), 0)),
            pl.BlockSpec((None, 8, LANES), lambda i: (layer, 0, 0)),
            *cast_in_specs,
        ],
        out_specs=[pl.BlockSpec((tm, wout), lambda i: (i, 0)), *cast_out_specs],
        out_shape=[jax.ShapeDtypeStruct((t_tok, wout), BF16), *cast_shapes],
        compiler_params=_params(("arbitrary",), 60),
        name="proj",
    )(x2d, mod, mod, g, w_in, rope, head_gains, *cast)
    return outs[0], outs[1:]


A_TQ = 512
KEY_CHUNK = 256


def _attn_a_kernel(q_ref, k_ref, v_ref, kc_ref, vc_ref, o_ref, kall_ref, vall_ref, s_ref, p_ref):
    kall_ref[:SEQ, :] = k_ref[0]
    kall_ref[SEQ:, :] = kc_ref[0]
    vall_ref[:SEQ, :] = v_ref[0]
    vall_ref[SEQ:, :] = vc_ref[0]
    units = [(r0, g) for r0 in range(0, SEQ, A_TQ) for g in range(2)]

    def scores(n):
        r0, g = units[n]
        s_ref[n % 2] = _dot_t(kall_ref[...], q_ref[0, r0:r0 + A_TQ, g * LANES:(g + 1) * LANES])

    scores(0)
    for n, (r0, g) in enumerate(units):
        if n + 1 < len(units):
            scores(n + 1)
        den = _staged_softmax(s_ref.at[n % 2], p_ref, KEY_CHUNK)
        o_t = _dot_tn(vall_ref[...], p_ref[...])
        o_ref[0, r0:r0 + A_TQ, g * LANES:(g + 1) * LANES] = (o_t / den).T.astype(BF16)


def _attn_a_call(px, pc, ctx_tiles):
    b = px.shape[0]
    ak, av = SEG["ak"], SEG["av"]
    akc, avc = _seg_block(ctx_tiles, "ak"), _seg_block(ctx_tiles, "av")
    return pl.pallas_call(
        _attn_a_kernel,
        grid=(b, 2),
        in_specs=[
            pl.BlockSpec((1, SEQ, 2 * LANES), lambda i, h: (i, 0, h)),
            pl.BlockSpec((1, SEQ, LANES), lambda i, h: (i, 0, ak + h)),
            pl.BlockSpec((1, SEQ, LANES), lambda i, h: (i, 0, av + h)),
            pl.BlockSpec((1, CTX_LEN, LANES), lambda i, h: (i, 0, akc + h)),
            pl.BlockSpec((1, CTX_LEN, LANES), lambda i, h: (i, 0, avc + h)),
        ],
        out_specs=pl.BlockSpec((1, SEQ, 2 * LANES), lambda i, h: (i, 0, h)),
        out_shape=jax.ShapeDtypeStruct((b, SEQ, GROUP_WIDTH), BF16),
        scratch_shapes=[pltpu.VMEM((SEQ + CTX_LEN, LANES), BF16), pltpu.VMEM((SEQ + CTX_LEN, LANES), BF16),
                        pltpu.VMEM((2, SEQ + CTX_LEN, A_TQ), F32), pltpu.VMEM((SEQ + CTX_LEN, A_TQ), BF16)],
        compiler_params=_params(("arbitrary",) * 2, 48),
        name="attn_a",
    )(px, px, px, pc, pc)


NB_ROWS = SEQ // GRID_W
NB_QROWS = 4
NB_KROWS = NB_QROWS + WIN_ROWS - 1
NB_BLOCKS = NB_ROWS // NB_QROWS
NB_TQ = NB_QROWS * GRID_W
NB_KEYS = NB_KROWS * GRID_W
NB_CHUNK = (NB_KEYS + CTX_LEN) // 3


def _nb_window_start(j):
    return min(max(j * NB_QROWS - WIN_ROWS // 2, 0), NB_ROWS - NB_KROWS)


def _nb_class(j):
    return 0 if j == 0 else (2 if j == NB_BLOCKS - 1 else 1)


def _na_bias_table(rpb):
    n_dr, n_dc = rpb.shape[-2:]
    rpb = rpb.astype(F32).reshape(-1, n_dr, n_dc)
    n_tab = rpb.shape[0]
    period = 2 * GRID_W
    fill = jnp.full((n_tab, n_dr, period - n_dc), NEG_INF, F32)
    ext = jnp.concatenate([rpb[..., WIN_COLS - 1::-1], fill, rpb[..., :WIN_COLS - 1:-1]], axis=-1)
    flat = jnp.tile(ext, (1, 1, GRID_W))[..., :GRID_W * (period - 1)]
    toep = flat.reshape(n_tab, n_dr, GRID_W, period - 1)[..., :GRID_W]
    c = jnp.arange(GRID_W)
    c_start = jnp.clip(c - WIN_COLS // 2, 0, GRID_W - WIN_COLS)
    kc = jnp.arange(GRID_W)
    in_win = (kc[:, None] >= c_start[None, :]) & (kc[:, None] < c_start[None, :] + WIN_COLS)
    toep = jnp.where(in_win, toep * LOG2E, NEG_INF)

    def masked_rows(n):
        return jnp.full((n_tab, n, GRID_W, GRID_W), NEG_INF, F32)

    classes = []
    for j in (0, 1, NB_BLOCKS - 1):
        q_rows = []
        for rq in range(NB_QROWS):
            r = j * NB_QROWS + rq
            r_start = min(max(r - WIN_ROWS // 2, 0), NB_ROWS - WIN_ROWS)
            lo = r_start - _nb_window_start(j)
            drow0 = r_start - r + WIN_ROWS - 1
            pieces = [masked_rows(lo), toep[:, drow0:drow0 + WIN_ROWS], masked_rows(NB_KROWS - WIN_ROWS - lo)]
            rows = jnp.concatenate([p for p in pieces if p.shape[1]], axis=1)
            q_rows.append(rows.reshape(n_tab, NB_KEYS, GRID_W))
        classes.append(jnp.concatenate(q_rows, axis=-1))
    return jnp.stack(classes, axis=1)


def _attn_b_kernel(q_ref, k_ref, v_ref, kc_ref, vc_ref, bias_ref, o_ref, s_ref, p_ref):
    def scores(j):
        q0, k0 = j * NB_TQ, _nb_window_start(j) * GRID_W
        q = q_ref[0, q0:q0 + NB_TQ, :]
        s_ref[j % 2, :NB_KEYS, :] = _dot_t(k_ref[0, k0:k0 + NB_KEYS, :], q) + bias_ref[0, _nb_class(j)]
        s_ref[j % 2, NB_KEYS:, :] = _dot_t(kc_ref[0], q)

    scores(0)
    for j in range(NB_BLOCKS):
        if j + 1 < NB_BLOCKS:
            scores(j + 1)
        q0, k0 = j * NB_TQ, _nb_window_start(j) * GRID_W
        den = _staged_softmax(s_ref.at[j % 2], p_ref, NB_CHUNK)
        o_t = (_dot_tn(v_ref[0, k0:k0 + NB_KEYS, :], p_ref[:NB_KEYS, :])
               + _dot_tn(vc_ref[0], p_ref[NB_KEYS:, :])) / den
        o_ref[0, q0:q0 + NB_TQ, :] = o_t.T.astype(BF16)


def _attn_b_call(px, pc, ctx_tiles, bias, layer):
    b = px.shape[0]
    bq, bk, bv = SEG["bq"], SEG["bk"], SEG["bv"]
    bkc, bvc = _seg_block(ctx_tiles, "bk"), _seg_block(ctx_tiles, "bv")
    return pl.pallas_call(
        _attn_b_kernel,
        grid=(b, 4),
        in_specs=[
            pl.BlockSpec((1, SEQ, LANES), lambda i, h: (i, 0, bq + h)),
            pl.BlockSpec((1, SEQ, LANES), lambda i, h: (i, 0, bk + h)),
            pl.BlockSpec((1, SEQ, LANES), lambda i, h: (i, 0, bv + h)),
            pl.BlockSpec((1, CTX_LEN, LANES), lambda i, h: (i, 0, bkc + h)),
            pl.BlockSpec((1, CTX_LEN, LANES), lambda i, h: (i, 0, bvc + h)),
            pl.BlockSpec((1, 3, NB_KEYS, NB_TQ), lambda i, h: (layer * 4 + h, 0, 0, 0)),
        ],
        out_specs=pl.BlockSpec((1, SEQ, LANES), lambda i, h: (i, 0, h)),
        out_shape=jax.ShapeDtypeStruct((b, SEQ, GROUP_WIDTH), BF16),
        scratch_shapes=[pltpu.VMEM((2, NB_KEYS + CTX_LEN, NB_TQ), F32),
                        pltpu.VMEM((NB_KEYS + CTX_LEN, NB_TQ), BF16)],
        compiler_params=_params(("arbitrary",) * 2, 32),
        name="attn_b",
    )(px, px, px, pc, pc, bias)


C_TQ = 512


def _diff_lambda(lam_ref, lam_init):
    a = jnp.sum(lam_ref[0:1, :] * lam_ref[1:2, :], axis=-1, keepdims=True)
    b = jnp.sum(lam_ref[2:3, :] * lam_ref[3:4, :], axis=-1, keepdims=True)
    return jnp.exp(a) - jnp.exp(b) + lam_init


def _diff_head(q, keys, vals, lam, gsub, lam_init):
    lane = lax.broadcasted_iota(jnp.int32, q.shape, 1)
    zero = jnp.zeros_like(q)
    maps = []
    for sub in range(2):
        qs = jnp.where((lane < DIFF_QK_DIM) == (sub == 0), q, zero)
        s = [_dot_t(k, qs) for k in keys]
        m = _colmax(s)
        e = [jnp.exp2(x - m) for x in s]
        maps.append((e, _colsum(e)))
    (e1, den1), (e2, den2) = maps
    ratio = lam * den1 / den2
    o_t = functools.reduce(jnp.add, [
        _dot_tn(v, (a - ratio * b).astype(BF16)) for a, b, v in zip(e1, e2, vals)]) / den1
    return _rms(o_t.T, gsub) * (1.0 - lam_init)


def _attn_c_kernel(lam_ref, gsub_ref, q_ref, k_ref, v_ref, kc_ref, vc_ref, o_ref,
                   kall_ref, vall_ref, s_ref, p_ref, *, lam_init):
    n_keys = SEQ + CTX_LEN
    kall_ref[:SEQ, :] = k_ref[0]
    kall_ref[SEQ:, :] = kc_ref[0]
    vall_ref[:SEQ, :] = v_ref[0]
    vall_ref[SEQ:, :] = vc_ref[0]
    lam = _diff_lambda(lam_ref, lam_init)
    lane = lax.broadcasted_iota(jnp.int32, (C_TQ, LANES), 1)
    tiles = list(range(0, SEQ, C_TQ))
    chunks = range(0, n_keys, KEY_CHUNK)

    def scores(n):
        q = q_ref[0, tiles[n]:tiles[n] + C_TQ, :]
        for sub in range(2):
            qs = jnp.where((lane < DIFF_QK_DIM) == (sub == 0), q, jnp.zeros_like(q))
            s_ref[n % 2, sub] = _dot_t(kall_ref[...], qs)

    scores(0)
    for n, r0 in enumerate(tiles):
        if n + 1 < len(tiles):
            scores(n + 1)
        slot = n % 2
        dens = []
        for sub in range(2):
            m8 = functools.reduce(jnp.maximum,
                                  [_fold8(s_ref[slot, sub, c0:c0 + KEY_CHUNK, :], jnp.max) for c0 in chunks])
            m = jnp.max(m8, axis=0, keepdims=True)
            den8 = jnp.zeros((SUBLANES, C_TQ), F32)
            for c0 in chunks:
                e = jnp.exp2(s_ref[slot, sub, c0:c0 + KEY_CHUNK, :] - m)
                den8 = den8 + _fold8(e, jnp.sum)
                s_ref[slot, sub, c0:c0 + KEY_CHUNK, :] = e
            dens.append(jnp.sum(den8, axis=0, keepdims=True))
        den1, den2 = dens
        ratio = lam * den1 / den2
        for c0 in chunks:
            p_ref[c0:c0 + KEY_CHUNK, :] = (s_ref[slot, 0, c0:c0 + KEY_CHUNK, :]
                                           - ratio * s_ref[slot, 1, c0:c0 + KEY_CHUNK, :]).astype(BF16)
        o_t = _dot_tn(vall_ref[...], p_ref[...]) / den1
        o = _rms(o_t.T, gsub_ref[...]) * (1.0 - lam_init)
        o_ref[0, r0:r0 + C_TQ, :] = o.astype(BF16)


def _attn_c_call(px, pc, ctx_tiles, lam_rows, gsub, layer, lam_init):
    b = px.shape[0]
    cq, ck, cv = SEG["cq"], SEG["ck"], SEG["cv"]
    ckc, cvc = _seg_block(ctx_tiles, "ck"), _seg_block(ctx_tiles, "cv")
    return pl.pallas_call(
        functools.partial(_attn_c_kernel, lam_init=lam_init),
        grid=(b, 4),
        in_specs=[
            pl.BlockSpec((None, 8, LANES), lambda i, h: (layer, 0, 0)),
            pl.BlockSpec((None, 1, LANES), lambda i, h: (layer, 0, 0)),
            pl.BlockSpec((1, SEQ, LANES), lambda i, h: (i, 0, cq + h)),
            pl.BlockSpec((1, SEQ, LANES), lambda i, h: (i, 0, ck + h)),
            pl.BlockSpec((1, SEQ, LANES), lambda i, h: (i, 0, cv + h)),
            pl.BlockSpec((1, CTX_LEN, LANES), lambda i, h: (i, 0, ckc + h)),
            pl.BlockSpec((1, CTX_LEN, LANES), lambda i, h: (i, 0, cvc + h)),
        ],
        out_specs=pl.BlockSpec((1, SEQ, LANES), lambda i, h: (i, 0, h)),
        out_shape=jax.ShapeDtypeStruct((b, SEQ, GROUP_WIDTH), BF16),
        scratch_shapes=[pltpu.VMEM((SEQ + CTX_LEN, LANES), BF16), pltpu.VMEM((SEQ + CTX_LEN, LANES), BF16),
                        pltpu.VMEM((2, 2, SEQ + CTX_LEN, C_TQ), F32), pltpu.VMEM((SEQ + CTX_LEN, C_TQ), BF16)],
        compiler_params=_params(("arbitrary",) * 2, 48),
        name="attn_c",
    )(lam_rows, gsub, px, px, px, pc, pc)


D_TQ = 256
D_SPAN = D_TQ + 2 * WINDOW


def _attn_d_kernel(sink_ref, q_ref, k_ref, v_ref, kc_ref, vc_ref, o_ref, *, layer):
    kvh = pl.program_id(1)
    kc, vc = kc_ref[0], vc_ref[0]
    row = lax.broadcasted_iota(jnp.int32, (2 * D_TQ, D_SPAN), 0)
    rel = (row & (D_TQ - 1)) - lax.broadcasted_iota(jnp.int32, (2 * D_TQ, D_SPAN), 1)
    head_row = lax.broadcasted_iota(jnp.int32, (2 * D_TQ, 1), 0)
    sink = jnp.where(head_row < D_TQ, sink_ref[layer, kvh * 2], sink_ref[layer, kvh * 2 + 1]) * LOG2E
    for i in range(SEQ // D_TQ):
        q0 = i * D_TQ
        k0 = min(max(q0 - WINDOW, 0), SEQ - D_SPAN)
        kw = k_ref[0, k0:k0 + D_SPAN, :]
        vw = v_ref[0, k0:k0 + D_SPAN, :]
        valid = jnp.abs(rel + (q0 - k0)) <= WINDOW
        q = jnp.concatenate([q_ref[0, q0:q0 + D_TQ, :LANES], q_ref[0, q0:q0 + D_TQ, LANES:]], axis=0)
        sw = jnp.where(valid, _dot_t(q, kw), NEG_INF)
        sc = _dot_t(q, kc)
        m = jnp.maximum(_rowmax([sw, sc]), sink)
        ew = jnp.exp2(sw - m)
        ec = jnp.exp2(sc - m)
        den = _rowsum([ew, ec]) + jnp.exp2(sink - m)
        o = ((_dot(ew.astype(BF16), vw) + _dot(ec.astype(BF16), vc)) / den).astype(BF16)
        o_ref[0, q0:q0 + D_TQ, :LANES] = o[:D_TQ]
        o_ref[0, q0:q0 + D_TQ, LANES:] = o[D_TQ:]


def _attn_d_call(px, pc, ctx_tiles, sink, layer):
    b = px.shape[0]
    dq, dk, dv = SEG["dq"], SEG["dk"], SEG["dv"]
    dkc, dvc = _seg_block(ctx_tiles, "dk"), _seg_block(ctx_tiles, "dv")
    return pl.pallas_call(
        functools.partial(_attn_d_kernel, layer=layer),
        grid=(b, 2),
        in_specs=[
            pl.BlockSpec(memory_space=pltpu.SMEM),
            pl.BlockSpec((1, SEQ, 2 * LANES), lambda i, h: (i, 0, dq // 2 + h)),
            pl.BlockSpec((1, SEQ, LANES), lambda i, h: (i, 0, dk + h)),
            pl.BlockSpec((1, SEQ, LANES), lambda i, h: (i, 0, dv + h)),
            pl.BlockSpec((1, CTX_LEN, LANES), lambda i, h: (i, 0, dkc + h)),
            pl.BlockSpec((1, CTX_LEN, LANES), lambda i, h: (i, 0, dvc + h)),
        ],
        out_specs=pl.BlockSpec((1, SEQ, 2 * LANES), lambda i, h: (i, 0, h)),
        out_shape=jax.ShapeDtypeStruct((b, SEQ, GROUP_WIDTH), BF16),
        compiler_params=_params(("arbitrary",) * 2, 32),
        name="attn_d",
    )(sink, px, px, px, pc, pc)


def _softmax_pv(q, k, v, sink=None):
    s = _dot_t(q, k)
    m = jnp.max(s, axis=-1, keepdims=True)
    if sink is not None:
        m = jnp.maximum(m, sink)
    e = jnp.exp2(s - m)
    den = jnp.sum(e, axis=-1, keepdims=True)
    if sink is not None:
        den = den + jnp.exp2(sink - m)
    return _dot(e.astype(BF16), v) / den


def _ctx_attn_kernel(sink_ref, lam_ref, gsub_ref, p_ref, o_ref, *, layer, lam_init):
    def blk(name, h):
        c = (SEG[name] + h) * LANES
        return p_ref[0, :, c:c + LANES]

    def put(group, h, o):
        c = group * GROUP_WIDTH + h * LANES
        o_ref[0, :, c:c + LANES] = o.astype(BF16)

    lam = _diff_lambda(lam_ref, lam_init)
    for h in range(4):
        put(0, h, _softmax_pv(blk("aq", h), blk("ak", h // 2), blk("av", h // 2)))
        put(1, h, _softmax_pv(blk("bq", h), blk("bk", h), blk("bv", h)))
        put(2, h, _diff_head(blk("cq", h), [blk("ck", h)], [blk("cv", h)], lam, gsub_ref[...], lam_init))
        put(3, h, _softmax_pv(blk("dq", h), blk("dk", h // 2), blk("dv", h // 2),
                              sink=sink_ref[layer, h] * LOG2E))


def _ctx_attn_call(pc, sink, lam_rows, gsub, layer, lam_init):
    b = pc.shape[0]
    return pl.pallas_call(
        functools.partial(_ctx_attn_kernel, layer=layer, lam_init=lam_init),
        grid=(b,),
        in_specs=[
            pl.BlockSpec(memory_space=pltpu.SMEM),
            pl.BlockSpec((None, 8, LANES), lambda i: (layer, 0, 0)),
            pl.BlockSpec((None, 1, LANES), lambda i: (layer, 0, 0)),
            pl.BlockSpec((1, CTX_LEN, PROJ_WIDTH), lambda i: (i, 0, 0)),
        ],
        out_specs=pl.BlockSpec((1, CTX_LEN, D_MODEL), lambda i: (i, 0, 0)),
        out_shape=jax.ShapeDtypeStruct((b, CTX_LEN, D_MODEL), BF16),
        compiler_params=_params(("arbitrary",), 32),
        name="ctx_attn",
    )(sink, lam_rows, gsub, pc)


POST_TM = 512
POST_TH = 1024


def _post_kernel(x_ref, oa_ref, ob_ref, oc_ref, od_ref, wout_ref, gate_mix_ref, shift_ref, scale_ref,
                 gate_mlp_ref, g_ref, wup_ref, wdn_ref, gfin_ref, out_ref, x1_ref, h_ref, *, final):
    j = pl.program_id(1)

    @pl.when(j == 0)
    def _():
        mix = functools.reduce(jnp.add, [
            _dot(o[...], wout_ref[n * GROUP_WIDTH:(n + 1) * GROUP_WIDTH, :])
            for n, o in enumerate((oa_ref, ob_ref, oc_ref, od_ref))])
        x1 = x_ref[...] + gate_mix_ref[0] * mix
        x1_ref[...] = x1
        h_ref[...] = (_rms(x1, g_ref[...]) * (1.0 + scale_ref[0]) + shift_ref[0]).astype(BF16)
        out_ref[...] = jnp.zeros_like(out_ref)

    u = _dot(h_ref[...], wup_ref[...])
    act = jnp.square(jnp.maximum(u, 0.0)).astype(BF16)
    out_ref[...] += _dot(act, wdn_ref[...])

    @pl.when(j == pl.num_programs(1) - 1)
    def _():
        y = x1_ref[...] + gate_mlp_ref[0] * out_ref[...]
        if final:
            y = _rms(y, gfin_ref[...])
        out_ref[...] = y


def _post_call(x2d, mix_parts, mix_blocks, w_out, mod, g_mlp, w_up, w_down, layer, g_final, *,
               mod_row, final):
    t_tok = x2d.shape[0]
    tm, th = POST_TM, POST_TH
    once = pl.Buffered(1)

    def mod_spec(k):
        return pl.BlockSpec((None, 1, 1, D_MODEL), lambda i, j: (layer, mod_row(i) * N_MOD + k, 0, 0))

    def part_spec(blk):
        return pl.BlockSpec((tm, GROUP_WIDTH), lambda i, j: (i, blk))

    return pl.pallas_call(
        functools.partial(_post_kernel, final=final),
        grid=(t_tok // tm, MLP_HIDDEN // th),
        in_specs=[
            pl.BlockSpec((tm, D_MODEL), lambda i, j: (i, 0)),
            *[part_spec(blk) for blk in mix_blocks],
            pl.BlockSpec((D_MODEL, D_MODEL), lambda i, j: (0, 0), pipeline_mode=once),
            mod_spec(2), mod_spec(3), mod_spec(4), mod_spec(5),
            pl.BlockSpec((None, 1, D_MODEL), lambda i, j: (layer, 0, 0)),
            pl.BlockSpec((D_MODEL, th), lambda i, j: (0, j)),
            pl.BlockSpec((th, D_MODEL), lambda i, j: (j, 0)),
            pl.BlockSpec((1, D_MODEL), lambda i, j: (0, 0)),
        ],
        out_specs=pl.BlockSpec((tm, D_MODEL), lambda i, j: (i, 0)),
        out_shape=jax.ShapeDtypeStruct((t_tok, D_MODEL), F32),
        scratch_shapes=[pltpu.VMEM((tm, D_MODEL), F32), pltpu.VMEM((tm, D_MODEL), BF16)],
        compiler_params=_params(("arbitrary", "arbitrary"), 60),
        name="post",
    )(x2d, *mix_parts, w_out, mod, mod, mod, mod, g_mlp, w_up, w_down, g_final)


def _rope_rows(n_tok, dim):
    t = jnp.arange(n_tok)
    row = (t // GRID_W).astype(F32)
    col = (t % GRID_W).astype(F32)
    n_freq = dim // 4
    inv_freq = ROPE_THETA ** (-jnp.arange(n_freq, dtype=F32) / n_freq)
    ang = jnp.concatenate([row[:, None] * inv_freq, col[:, None] * inv_freq], axis=-1)
    cos = jnp.repeat(jnp.cos(ang), 2, axis=-1)
    sin = jnp.repeat(jnp.sin(ang), 2, axis=-1)
    even = (jnp.arange(dim) % 2) == 0
    parts = [cos, jnp.where(even, -sin, 0.0), jnp.where(even, 0.0, sin)]
    return [jnp.tile(p, (1, LANES // dim)) for p in parts]


def _rope_table(n_tok):
    return jnp.concatenate(_rope_rows(n_tok, HEAD_DIM) + _rope_rows(n_tok, DIFF_QK_DIM), axis=-1)


def _identity_rope_table(n_tok):
    one = jnp.ones((n_tok, LANES), F32)
    zero = jnp.zeros((n_tok, LANES), F32)
    return jnp.concatenate([one, zero, zero] * 2, axis=-1)


def _pad_rows(rows, n_rows=8):
    depth = rows[0].shape[0]
    padded = [jnp.pad(r.astype(F32), ((0, 0), (0, LANES - r.shape[1]))) for r in rows]
    padded += [jnp.zeros((depth, LANES), F32)] * (n_rows - len(rows))
    return jnp.stack(padded, axis=1)


def kernel(x, c, ctx, c_ctx, g_mix, g_mlp, w_mod, b_mod, w_in, w_out, gqa_gq, gqa_gk, na_rpb,
           diff_lq1, diff_lk1, diff_lq2, diff_lk2, diff_gsub, swa_sink, w_up, w_down, g_final):
    b, s, d = x.shape
    assert (s, d) == (SEQ, D_MODEL) and ctx.shape == (b, CTX_LEN, D_MODEL) and b <= CTX_MOD_ROW

    c_rows = jnp.concatenate(
        [c, jnp.zeros((CTX_MOD_ROW - b, D_MODEL), F32), c_ctx[None],
         jnp.zeros((MOD_ROWS - CTX_MOD_ROW - 1, D_MODEL), F32)], axis=0)
    mod = _mod_call(c_rows, w_mod, b_mod).reshape(DEPTH, MOD_ROWS * N_MOD, 1, D_MODEL)

    w_in_b = w_in.astype(BF16)

    rope_x = _rope_table(SEQ)
    rope_c = _identity_rope_table(PROJ_TM)
    x_tiles_per_seq = SEQ // PROJ_TM
    x_post_tiles_per_seq = SEQ // POST_TM
    g_final2 = g_final.reshape(1, D_MODEL)
    g_mix3 = g_mix.reshape(DEPTH, 1, D_MODEL)
    g_mlp3 = g_mlp.reshape(DEPTH, 1, D_MODEL)
    head_gains = _pad_rows([gqa_gq, gqa_gk])
    lam_rows = _pad_rows([diff_lq1, diff_lk1, diff_lq2, diff_lk2])
    gsub = diff_gsub.reshape(DEPTH, 1, HEAD_DIM)
    sink = swa_sink.astype(F32)
    nb_bias = _na_bias_table(na_rpb)

    x2d = x.reshape(b * s, D_MODEL)
    c2d = ctx.reshape(b * CTX_LEN, D_MODEL)
    for l in range(DEPTH):
        with_ctx = l < DEPTH - 1
        lam_init = 0.8 - 0.6 * math.exp(-0.3 * l)
        ctx_tiles = ALL_TILES if with_ctx else KV_TILES

        px, (w_out_b, w_up_b, w_down_b) = _proj_call(
            x2d, mod, g_mix3, w_in_b, l, rope_x, head_gains, tiles=ALL_TILES, cast=(w_out, w_up, w_down),
            mod_row=lambda i: i // x_tiles_per_seq, rope_tile=lambda i: i % x_tiles_per_seq)
        pc, _ = _proj_call(c2d, mod, g_mix3, w_in_b, l, rope_c, head_gains, tiles=ctx_tiles,
                           mod_row=lambda i: CTX_MOD_ROW, rope_tile=lambda i: 0)
        px = px.reshape(b, SEQ, PROJ_WIDTH)
        pc = pc.reshape(b, CTX_LEN, len(ctx_tiles) * COL_TILE)

        oa = _attn_a_call(px, pc, ctx_tiles)
        ob = _attn_b_call(px, pc, ctx_tiles, nb_bias, l)
        oc = _attn_c_call(px, pc, ctx_tiles, lam_rows, gsub, l, lam_init)
        od = _attn_d_call(px, pc, ctx_tiles, sink, l)
        parts = [o.reshape(b * s, GROUP_WIDTH) for o in (oa, ob, oc, od)]
        if with_ctx:
            o_ctx = _ctx_attn_call(pc, sink, lam_rows, gsub, l, lam_init).reshape(b * CTX_LEN, D_MODEL)

        x2d = _post_call(x2d, parts, (0, 0, 0, 0), w_out_b, mod, g_mlp3, w_up_b, w_down_b, l, g_final2,
                         mod_row=lambda i: i // x_post_tiles_per_seq, final=not with_ctx)
        if with_ctx:
            c2d = _post_call(c2d, [o_ctx] * 4, (0, 1, 2, 3), w_out_b, mod, g_mlp3, w_up_b, w_down_b, l,
                             g_final2, mod_row=lambda i: CTX_MOD_ROW, final=False)
    return x2d.reshape(b, s, D_MODEL)
```
